```python
import jax, jax.numpy as jnp
from jax import lax
import numpy as np

D_MODEL = 2048
BATCH = 1
SEQ = 8192
DEPTH = 2

D_MIX = D_MODEL
GROUP = D_MIX // 4
MLA_HEADS = 4
MLA_NOPE = 128
MLA_ROPE = 64
MLA_V = GROUP // MLA_HEADS
MLA_QK = MLA_NOPE + MLA_ROPE
Q_LORA = 512
KV_LORA = 256
ROPE_BASE = 10000.0
ATT_BLOCK = 128
SC_WIDTH = GROUP
SC_K = 3
CF_WIDTH = GROUP
CF_K = 31
RET_HEADS = 4
RET_DV = GROUP // RET_HEADS
RET_DK = RET_DV // 2
RET_CHUNK = 128
D_FF = 5632
FFN_K = 3
PLE_DIM = 256
EPS = 1e-6

IN_SIZES = (Q_LORA, KV_LORA, MLA_ROPE,
            SC_WIDTH, SC_WIDTH, SC_WIDTH,
            CF_WIDTH, CF_WIDTH,
            RET_HEADS * RET_DK, RET_HEADS * RET_DK,
            RET_HEADS * RET_DV, RET_HEADS * RET_DV)
N_IN = sum(IN_SIZES)

kernel_name = "hybrid_parallel_mixer_block"

F32 = jnp.float32


def split_cols(z, sizes):
    idx = np.cumsum(np.array(sizes))[:-1].tolist()
    return jnp.split(z, idx, axis=-1)


def rmsnorm(x, g):
    xf = x.astype(F32)
    y = xf * lax.rsqrt(jnp.mean(xf * xf, axis=-1, keepdims=True) + EPS)
    return (y * g.astype(F32)).astype(x.dtype)


def layernorm(x, g, b):
    xf = x.astype(F32)
    mu = jnp.mean(xf, axis=-1, keepdims=True)
    xc = xf - mu
    y = xc * lax.rsqrt(jnp.mean(xc * xc, axis=-1, keepdims=True) + EPS)
    return (y * g.astype(F32) + b.astype(F32)).astype(x.dtype)


def causal_dwconv(x, w, b=None):
    K, C = w.shape
    y = lax.conv_general_dilated(x, w[:, None, :].astype(x.dtype), window_strides=(1,),
                                 padding=[(K - 1, 0)], dimension_numbers=('NWC', 'WIO', 'NWC'),
                                 feature_group_count=C)
    if b is not None:
        y = y + b.astype(y.dtype)
    return y


def rotate(x, pos, inv_freq):
    ang = pos.astype(F32)[:, :, None] * inv_freq[None, None, :]
    cos = jnp.cos(ang)[:, :, None, :]
    sin = jnp.sin(ang)[:, :, None, :]
    x1, x2 = jnp.split(x.astype(F32), 2, axis=-1)
    return jnp.concatenate([x1 * cos - x2 * sin, x2 * cos + x1 * sin], axis=-1).astype(x.dtype)


def causal_attention_blocked(q, k, v):
    B, S, H, Dq = q.shape
    Dv = v.shape[-1]
    nb = S // ATT_BLOCK
    scale = Dq ** -0.5
    qb = q.reshape(B, nb, ATT_BLOCK, H, Dq).transpose(1, 0, 2, 3, 4)
    k_idx = jnp.arange(S)

    def one_block(args):
        blk, qblk = args
        q_idx = blk * ATT_BLOCK + jnp.arange(ATT_BLOCK)
        s = jnp.einsum('bqhd,bkhd->bhqk', qblk, k, preferred_element_type=F32) * scale
        s = jnp.where(k_idx[None, :] <= q_idx[:, None], s, jnp.finfo(F32).min)
        pr = jax.nn.softmax(s, axis=-1)
        return jnp.einsum('bhqk,bkhd->bqhd', pr.astype(v.dtype), v)

    out = lax.map(one_block, (jnp.arange(nb), qb))
    return out.transpose(1, 0, 2, 3, 4).reshape(B, S, H, Dv)


def retention_chunkwise(q, k, v):
    B, S, H, dk = q.shape
    dv = v.shape[-1]
    C = RET_CHUNK
    n = S // C
    to_chunks = lambda t: t.astype(F32).reshape(B, n, C, H, t.shape[-1]).transpose(1, 0, 3, 2, 4)
    qc, kc, vc = to_chunks(q), to_chunks(k), to_chunks(v)
    lg = jnp.log(1.0 - 2.0 ** (-5.0 - jnp.arange(H, dtype=F32)))[:, None]
    idx = jnp.arange(C, dtype=F32)
    rel = idx[:, None] - idx[None, :]
    inner_decay = jnp.where(rel[None] >= 0, jnp.exp(lg[:, :, None] * rel[None]), 0.0)
    q_decay = jnp.exp(lg * (idx + 1.0))[None, :, :, None]
    k_decay = jnp.exp(lg * (C - 1.0 - idx))[None, :, :, None]
    chunk_decay = jnp.exp(lg[:, 0] * C)[None, :, None, None]

    def step(state, inp):
        qb, kb, vb = inp
        sc = jnp.einsum('bhqd,bhkd->bhqk', qb, kb) * inner_decay[None]
        inner = jnp.einsum('bhqk,bhkv->bhqv', sc, vb)
        cross = jnp.einsum('bhqd,bhdv->bhqv', qb, state) * q_decay
        new_state = state * chunk_decay + jnp.einsum('bhkd,bhkv->bhdv', kb * k_decay, vb)
        return new_state, inner + cross

    state0 = jnp.zeros((B, H, dk, dv), F32)
    _, out = lax.scan(step, state0, (qc, kc, vc))
    return out.transpose(1, 0, 3, 2, 4).reshape(B, S, H, dv)


def mla_group(zq, zkv, zkr, pos, g_qa, g_kva, w_q_up, w_kv_up, g_qn, g_kn):
    B, S, _ = zq.shape
    c_q = rmsnorm(zq, g_qa)
    c_kv = rmsnorm(zkv, g_kva)
    q = (c_q @ w_q_up).reshape(B, S, MLA_HEADS, MLA_QK)
    kv = (c_kv @ w_kv_up).reshape(B, S, MLA_HEADS, MLA_NOPE + MLA_V)
    k_nope, v = kv[..., :MLA_NOPE], kv[..., MLA_NOPE:]
    k_rope = jnp.broadcast_to(zkr[:, :, None, :], (B, S, MLA_HEADS, MLA_ROPE))
    k = jnp.concatenate([k_nope, k_rope], axis=-1)
    q = rmsnorm(q, g_qn)
    k = rmsnorm(k, g_kn)
    inv = ROPE_BASE ** (-jnp.arange(0, MLA_ROPE, 2, dtype=F32) / MLA_ROPE)
    q = jnp.concatenate([q[..., :MLA_NOPE], rotate(q[..., MLA_NOPE:], pos, inv)], axis=-1)
    k = jnp.concatenate([k[..., :MLA_NOPE], rotate(k[..., MLA_NOPE:], pos, inv)], axis=-1)
    o = causal_attention_blocked(q, k, v)
    return o.reshape(B, S, MLA_HEADS * MLA_V)


def retention_group(rq, rk, rv, rg, pos, g_ret):
    B, S, _ = rq.shape
    inv = 1.0 / (10000.0 ** jnp.linspace(0.0, 1.0, RET_DK // 2, dtype=F32))
    q = rotate(rq.reshape(B, S, RET_HEADS, RET_DK), pos, inv)
    k = rotate(rk.reshape(B, S, RET_HEADS, RET_DK), pos, inv) * (RET_DK ** -0.5)
    v = rv.reshape(B, S, RET_HEADS, RET_DV)
    o = retention_chunkwise(q, k, v)
    o = rmsnorm(o, g_ret).astype(rv.dtype)
    return jax.nn.silu(rg) * o.reshape(B, S, RET_HEADS * RET_DV)


def setup_inputs(seed: int = 0) -> dict:
    key = jax.random.key(seed)
    ks = jax.random.split(key, 32)
    L = DEPTH

    def w(i, shape, fan_in):
        return jax.random.normal(ks[i], shape, F32) * (fan_in ** -0.5)

    def gain(i, shape):
        return 1.0 + 0.02 * jax.random.normal(ks[i], shape, F32)

    def bias(i, shape):
        return 0.02 * jax.random.normal(ks[i], shape, F32)

    x = jax.random.normal(ks[0], (BATCH, SEQ, D_MODEL), F32)
    p = jax.random.normal(ks[1], (DEPTH, BATCH, SEQ, PLE_DIM), F32)
    offset = jax.random.randint(ks[2], (BATCH, 1), 0, 1024, dtype=jnp.int32)
    positions = jnp.arange(SEQ, dtype=jnp.int32)[None, :] + offset
    return {
        "x": x, "p": p, "positions": positions,
        "g_mix": gain(3, (L, D_MODEL)),
        "w_in": w(4, (L, D_MODEL, N_IN), D_MODEL),
        "g_qa": gain(5, (L, Q_LORA)),
        "g_kva": gain(6, (L, KV_LORA)),
        "w_q_up": w(7, (L, Q_LORA, MLA_HEADS * MLA_QK), Q_LORA),
        "w_kv_up": w(8, (L, KV_LORA, MLA_HEADS * (MLA_NOPE + MLA_V)), KV_LORA),
        "g_qn": gain(9, (L, MLA_QK)),
        "g_kn": gain(10, (L, MLA_QK)),
        "w_sc": w(11, (L, SC_K, SC_WIDTH), SC_K),
        "w_cf": w(12, (L, CF_K, CF_WIDTH), CF_K),
        "b_cf": bias(13, (L, CF_WIDTH)),
        "g_cf_ln": gain(14, (L, CF_WIDTH)),
        "b_cf_ln": bias(15, (L, CF_WIDTH)),
        "g_ret": gain(16, (L, RET_HEADS, RET_DV)),
        "w_o": w(17, (L, D_MIX, D_MODEL), D_MIX),
        "g_ffn": gain(18, (L, D_MODEL)),
        "w_up": w(19, (L, D_MODEL, 2 * D_FF), D_MODEL),
        "w_ffn_conv": w(20, (L, FFN_K, 2 * D_FF), FFN_K),
        "w_down": w(21, (L, D_FF, D_MODEL), D_FF),
        "g_pe": gain(22, (L, D_MODEL)),
        "w_pe": w(23, (L, PLE_DIM, D_MODEL), PLE_DIM),
        "w_pg": w(24, (L, D_MODEL, D_MODEL), D_MODEL),
    }


def reference(x, p, positions, g_mix, w_in, g_qa, g_kva, w_q_up, w_kv_up, g_qn, g_kn,
              w_sc, w_cf, b_cf, g_cf_ln, b_cf_ln, g_ret, w_o, g_ffn, w_up, w_ffn_conv,
              w_down, g_pe, w_pe, w_pg):
    for i in range(DEPTH):
        h = rmsnorm(x, g_mix[i])
        z = h @ w_in[i]
        (zq, zkv, zkr, sc_b, sc_c, sc_h, cf_a, cf_g, rq, rk, rv, rg) = split_cols(z, IN_SIZES)
        y_a = mla_group(zq, zkv, zkr, positions, g_qa[i], g_kva[i], w_q_up[i], w_kv_up[i],
                        g_qn[i], g_kn[i])
        y_b = sc_b * causal_dwconv(sc_c * sc_h, w_sc[i])
        u = causal_dwconv(cf_a * jax.nn.sigmoid(cf_g), w_cf[i], b_cf[i])
        y_c = jax.nn.silu(layernorm(u, g_cf_ln[i], b_cf_ln[i]))
        y_d = retention_group(rq, rk, rv, rg, positions, g_ret[i])
        y = jnp.concatenate([y_a, y_b, y_c, y_d], axis=-1) @ w_o[i]
        x = x + y
        uf = causal_dwconv(rmsnorm(x, g_ffn[i]) @ w_up[i], w_ffn_conv[i])
        gate, up = jnp.split(uf, 2, axis=-1)
        x = x + (jax.nn.silu(gate) * up) @ w_down[i]
        pe_gate = jax.nn.sigmoid(rmsnorm(x, g_pe[i]) @ w_pg[i])
        x = x + (p[i] @ w_pe[i]) * pe_gate
    return x
```

```python
import functools

import numpy as np
import jax
import jax.numpy as jnp
from jax import lax
from jax.experimental import pallas as pl
from jax.experimental.pallas import tpu as pltpu

F32 = jnp.float32
BF16 = jnp.bfloat16

D_MODEL = 2048
SEQ = 8192
DEPTH = 2
GROUP = 512
MLA_HEADS = 4
MLA_NOPE = 128
MLA_ROPE = 64
MLA_V = 128
MLA_QK = MLA_NOPE + MLA_ROPE
Q_LORA = 512
KV_LORA = 256
ROPE_BASE = 10000.0
SC_K = 3
CF_K = 31
RET_HEADS = 4
RET_DV = 128
RET_DK = 64
RET_CHUNK = 128
D_FF = 5632
FFN_K = 3
PLE_DIM = 256
EPS = 1e-6

LANES = 128
SUBLANES = 8
HEAD_PAD = 2 * LANES
ROW_TILE = 512
FF_TILE = 512
ATT_TILE = 512
CF_HALO = 32
VMEM_LIMIT = 56 * 1024 * 1024
MASK_VALUE = -1e30


def _params(sem):
    return pltpu.CompilerParams(dimension_semantics=sem, vmem_limit_bytes=VMEM_LIMIT)


def _rms_scale(xf, n):
    return lax.rsqrt(jnp.sum(xf * xf, axis=-1, keepdims=True) * (1.0 / n) + EPS)


def _normed_bf16(x_ref, g_ref):
    xf = x_ref[...]
    return (xf * _rms_scale(xf, D_MODEL) * g_ref[...]).astype(BF16)


def _dot(a, b):
    return jnp.dot(a, b, preferred_element_type=F32)


def _rope_slab(x, cos_t, sin_a, sin_b):
    return (x * cos_t + pltpu.roll(x, LANES - 32, axis=1) * sin_a
            + pltpu.roll(x, 32, axis=1) * sin_b)


def _mla_pre_kernel(x_ref, pos_ref, gmix_ref, wa_ref, gqa_ref, gkva_ref, wq_ref, wkv_ref,
                    gq_ref, gkn_ref, gkr_ref, freq_ref, q_out, k_out, v_out):
    h = _normed_bf16(x_ref, gmix_ref)
    z = _dot(h, wa_ref[...])
    zq = z[:, :Q_LORA]
    zkv = z[:, Q_LORA:Q_LORA + KV_LORA]
    zkr = z[:, Q_LORA + KV_LORA:]
    c_q = (zq * _rms_scale(zq, Q_LORA) * gqa_ref[...]).astype(BF16)
    c_kv = (zkv * _rms_scale(zkv, KV_LORA) * gkva_ref[...]).astype(BF16)

    ang = pos_ref[...] * freq_ref[...]
    lane = lax.broadcasted_iota(jnp.int32, ang.shape, 1)
    cos_t = jnp.cos(ang)
    sin_t = jnp.sin(ang)
    sin_a = jnp.where(lane < 32, -sin_t, 0.0)
    sin_b = jnp.where((lane >= 32) & (lane < 64), sin_t, 0.0)

    q = _dot(c_q, wq_ref[...])
    kv = _dot(c_kv, wkv_ref[...])
    v_out[...] = kv[:, MLA_HEADS * MLA_NOPE:].astype(BF16)

    kr = zkr * gkr_ref[...]
    ss_kr = jnp.sum(zkr * zkr, axis=-1, keepdims=True)
    kr_rot = _rope_slab(kr, cos_t, sin_a, sin_b)
    for hd in range(MLA_HEADS):
        qb = q[:, hd * HEAD_PAD:(hd + 1) * HEAD_PAD]
        qn = qb * _rms_scale(qb, MLA_QK) * gq_ref[...]
        q_out[:, hd * HEAD_PAD:hd * HEAD_PAD + LANES] = qn[:, :LANES].astype(BF16)
        q_out[:, hd * HEAD_PAD + LANES:(hd + 1) * HEAD_PAD] = _rope_slab(
            qn[:, LANES:], cos_t, sin_a, sin_b).astype(BF16)
        kn = kv[:, hd * MLA_NOPE:(hd + 1) * MLA_NOPE]
        r_k = lax.rsqrt((jnp.sum(kn * kn, axis=-1, keepdims=True) + ss_kr) * (1.0 / MLA_QK) + EPS)
        k_out[:, hd * HEAD_PAD:hd * HEAD_PAD + LANES] = (kn * r_k * gkn_ref[...]).astype(BF16)
        k_out[:, hd * HEAD_PAD + LANES:(hd + 1) * HEAD_PAD] = (kr_rot * r_k).astype(BF16)


def _mla_pre(x, posf, g_mix, wa, g_qa, g_kva, wq, wkv, gq, gkn, gkr, freq):
    tm = ROW_TILE
    row = lambda n: pl.BlockSpec((tm, n), lambda i: (i, 0))
    full = lambda a: pl.BlockSpec(a.shape, lambda i: (0, 0))
    return pl.pallas_call(
        _mla_pre_kernel,
        grid=(SEQ // tm,),
        in_specs=[row(D_MODEL), row(1)] + [full(a) for a in
                                           (g_mix, wa, g_qa, g_kva, wq, wkv, gq, gkn, gkr, freq)],
        out_specs=[row(MLA_HEADS * HEAD_PAD), row(MLA_HEADS * HEAD_PAD), row(MLA_HEADS * MLA_V)],
        out_shape=[jax.ShapeDtypeStruct((SEQ, MLA_HEADS * HEAD_PAD), BF16),
                   jax.ShapeDtypeStruct((SEQ, MLA_HEADS * HEAD_PAD), BF16),
                   jax.ShapeDtypeStruct((SEQ, MLA_HEADS * MLA_V), BF16)],
        compiler_params=_params(("parallel",)),
        name="mla_pre",
    )(x, posf, g_mix, wa, g_qa, g_kva, wq, wkv, gq, gkn, gkr, freq)


def _flash_kernel(q_ref, k_ref, v_ref, o_ref, m_scr, l_scr, acc_scr):
    t = ATT_TILE
    i = pl.program_id(1)
    m_scr[...] = jnp.full(m_scr.shape, MASK_VALUE, F32)
    l_scr[...] = jnp.zeros(l_scr.shape, F32)
    acc_scr[...] = jnp.zeros(acc_scr.shape, F32)
    q = q_ref[...]

    def block(j, masked):
        start = pl.multiple_of(j * t, t)
        k = k_ref[pl.ds(start, t), :]
        v = v_ref[pl.ds(start, t), :]
        s = lax.dot_general(q, k, (((1,), (1,)), ((), ())), preferred_element_type=F32)
        if masked:
            r = lax.broadcasted_iota(jnp.int32, s.shape, 0)
            c = lax.broadcasted_iota(jnp.int32, s.shape, 1)
            s = jnp.where(c <= r, s, MASK_VALUE)
        m_prev = m_scr[...]
        m_new = jnp.maximum(m_prev, jnp.max(s, axis=-1, keepdims=True))
        alpha = jnp.exp(m_prev - m_new)
        p = jnp.exp(s - m_new[:, :1])
        l_scr[...] = alpha * l_scr[...] + jnp.sum(p, axis=-1, keepdims=True)
        acc_scr[...] = alpha * acc_scr[...] + _dot(p.astype(BF16), v)
        m_scr[...] = m_new

    def body(j, carry):
        block(j, False)
        return carry

    lax.fori_loop(0, i, body, 0)
    block(i, True)
    o_ref[...] = (acc_scr[...] / l_scr[...]).astype(o_ref.dtype)


def _flash(q, k, v):
    t = ATT_TILE
    return pl.pallas_call(
        _flash_kernel,
        grid=(MLA_HEADS, SEQ // t),
        in_specs=[pl.BlockSpec((t, HEAD_PAD), lambda h, i: (i, h)),
                  pl.BlockSpec((SEQ, HEAD_PAD), lambda h, i: (0, h)),
                  pl.BlockSpec((SEQ, MLA_V), lambda h, i: (0, h))],
        out_specs=pl.BlockSpec((t, MLA_V), lambda h, i: (i, h)),
        out_shape=jax.ShapeDtypeStruct((SEQ, MLA_HEADS * MLA_V), BF16),
        scratch_shapes=[pltpu.VMEM((t, LANES), F32), pltpu.VMEM((t, LANES), F32),
                        pltpu.VMEM((t, MLA_V), F32)],
        compiler_params=_params(("parallel", "arbitrary")),
        name="flash",
    )(q, k, v)


def _sconv_kernel(x_ref, gmix_ref, w_ref, wsc_ref, o_ref, buf):
    tm = ROW_TILE
    h = _normed_bf16(x_ref, gmix_ref)
    z = _dot(h, w_ref[...])

    @pl.when(pl.program_id(0) == 0)
    def _():
        buf[0:SUBLANES, :] = jnp.zeros((SUBLANES, GROUP), F32)

    g = z[:, GROUP:2 * GROUP] * z[:, 2 * GROUP:]
    buf[SUBLANES:, :] = g
    w = wsc_ref[...]
    conv = (w[2:3, :] * g + w[1:2, :] * buf[SUBLANES - 1:SUBLANES - 1 + tm, :]
            + w[0:1, :] * buf[SUBLANES - 2:SUBLANES - 2 + tm, :])
    o_ref[...] = (z[:, :GROUP] * conv).astype(o_ref.dtype)
    buf[0:SUBLANES, :] = buf[tm:tm + SUBLANES, :]


def _sconv(x, g_mix, w, wsc):
    tm = ROW_TILE
    return pl.pallas_call(
        _sconv_kernel,
        grid=(SEQ // tm,),
        in_specs=[pl.BlockSpec((tm, D_MODEL), lambda i: (i, 0)),
                  pl.BlockSpec(g_mix.shape, lambda i: (0, 0)),
                  pl.BlockSpec(w.shape, lambda i: (0, 0)),
                  pl.BlockSpec(wsc.shape, lambda i: (0, 0))],
        out_specs=pl.BlockSpec((tm, GROUP), lambda i: (i, 0)),
        out_shape=jax.ShapeDtypeStruct((SEQ, GROUP), BF16),
        scratch_shapes=[pltpu.VMEM((tm + SUBLANES, GROUP), F32)],
        compiler_params=_params(("arbitrary",)),
        name="sconv",
    )(x, g_mix, w, wsc)


def _conformer_kernel(x_ref, gmix_ref, w_ref, wcf_ref, bcf_ref, gln_ref, bln_ref, o_ref, buf):
    tm = ROW_TILE
    h = _normed_bf16(x_ref, gmix_ref)
    z = _dot(h, w_ref[...])

    @pl.when(pl.program_id(0) == 0)
    def _():
        buf[0:CF_HALO, :] = jnp.zeros((CF_HALO, GROUP), F32)

    buf[CF_HALO:, :] = z[:, :GROUP] * jax.nn.sigmoid(z[:, GROUP:])
    w = wcf_ref[...]
    base = CF_HALO - (CF_K - 1)
    u = bcf_ref[...] + w[0:1, :] * buf[base:base + tm, :]
    for j in range(1, CF_K):
        u = u + w[j:j + 1, :] * buf[base + j:base + j + tm, :]
    mu = jnp.mean(u, axis=-1, keepdims=True)
    uc = u - mu
    y = uc * lax.rsqrt(jnp.mean(uc * uc, axis=-1, keepdims=True) + EPS)
    y = y * gln_ref[...] + bln_ref[...]
    o_ref[...] = (y * jax.nn.sigmoid(y)).astype(o_ref.dtype)
    buf[0:CF_HALO, :] = buf[tm:tm + CF_HALO, :]


def _conformer(x, g_mix, w, wcf, bcf, gln, bln):
    tm = ROW_TILE
    full = lambda a: pl.BlockSpec(a.shape, lambda i: (0, 0))
    return pl.pallas_call(
        _conformer_kernel,
        grid=(SEQ // tm,),
        in_specs=[pl.BlockSpec((tm, D_MODEL), lambda i: (i, 0))]
        + [full(a) for a in (g_mix, w, wcf, bcf, gln, bln)],
        out_specs=pl.BlockSpec((tm, GROUP), lambda i: (i, 0)),
        out_shape=jax.ShapeDtypeStruct((SEQ, GROUP), BF16),
        scratch_shapes=[pltpu.VMEM((tm + CF_HALO, GROUP), F32)],
        compiler_params=_params(("arbitrary",)),
        name="conformer",
    )(x, g_mix, w, wcf, bcf, gln, bln)


def _retention_kernel(x_ref, pos_ref, gmix_ref, w_ref, freq_ref, qmask_ref, kdec_ref, qdec_ref,
                      inner_ref, cdec_ref, gret_ref, o_ref, state):
    tm = ROW_TILE
    c = RET_CHUNK
    dk2 = RET_HEADS * RET_DK
    h = _normed_bf16(x_ref, gmix_ref)
    z = _dot(h, w_ref[...])

    @pl.when(pl.program_id(0) == 0)
    def _():
        state[...] = jnp.zeros(state.shape, F32)

    ang = pos_ref[...] * freq_ref[...]
    cos_t = jnp.cos(ang)
    sin_t = jnp.sin(ang)

    def rot(t):
        t1, t2 = t[:, :LANES], t[:, LANES:]
        return jnp.concatenate([t1 * cos_t - t2 * sin_t, t2 * cos_t + t1 * sin_t], axis=1)

    q = rot(z[:, :dk2])
    k = rot(z[:, dk2:2 * dk2]) * (RET_DK ** -0.5)
    v = z[:, 2 * dk2:2 * dk2 + GROUP]
    gate = z[:, 2 * dk2 + GROUP:]

    for ci in range(tm // c):
        rows = slice(ci * c, (ci + 1) * c)
        qc = q[rows]
        kt = k[rows].T
        kt_b = kt.astype(BF16)
        for hd in range(RET_HEADS):
            cols = slice(hd * RET_DV, (hd + 1) * RET_DV)
            qm = (qc * qmask_ref[hd]).astype(BF16)
            vh = v[rows, cols].astype(BF16)
            sc = _dot(qm, kt_b) * inner_ref[hd]
            st = state[hd]
            o = _dot(sc.astype(BF16), vh) + _dot(qm, st.astype(BF16)) * qdec_ref[hd]
            state[hd] = st * cdec_ref[hd] + _dot((kt * kdec_ref[hd]).astype(BF16), vh)
            on = o * _rms_scale(o, RET_DV) * gret_ref[hd]
            gh = gate[rows, cols]
            o_ref[rows, cols] = (gh * jax.nn.sigmoid(gh) * on).astype(o_ref.dtype)


def _retention(x, posf, g_mix, w, freq, qmask, kdec, qdec, inner, cdec, gret):
    tm = ROW_TILE
    full = lambda a: pl.BlockSpec(a.shape, lambda i: (0,) * a.ndim)
    return pl.pallas_call(
        _retention_kernel,
        grid=(SEQ // tm,),
        in_specs=[pl.BlockSpec((tm, D_MODEL), lambda i: (i, 0)),
                  pl.BlockSpec((tm, 1), lambda i: (i, 0))]
        + [full(a) for a in (g_mix, w, freq, qmask, kdec, qdec, inner, cdec, gret)],
        out_specs=pl.BlockSpec((tm, GROUP), lambda i: (i, 0)),
        out_shape=jax.ShapeDtypeStruct((SEQ, GROUP), BF16),
        scratch_shapes=[pltpu.VMEM((RET_HEADS, RET_HEADS * RET_DK, RET_DV), F32)],
        compiler_params=_params(("arbitrary",)),
        name="retention",
    )(x, posf, g_mix, w, freq, qmask, kdec, qdec, inner, cdec, gret)


def _out_proj_kernel(x_ref, ya_ref, yb_ref, yc_ref, yd_ref, wo_ref, o_ref):
    acc = x_ref[...]
    for n, y_ref in enumerate((ya_ref, yb_ref, yc_ref, yd_ref)):
        acc = acc + _dot(y_ref[...], wo_ref[n * GROUP:(n + 1) * GROUP, :])
    o_ref[...] = acc


def _out_proj(x, ya, yb, yc, yd, wo):
    tm = ROW_TILE
    row = lambda n: pl.BlockSpec((tm, n), lambda i: (i, 0))
    return pl.pallas_call(
        _out_proj_kernel,
        grid=(SEQ // tm,),
        in_specs=[row(D_MODEL), row(GROUP), row(GROUP), row(GROUP), row(GROUP),
                  pl.BlockSpec(wo.shape, lambda i: (0, 0))],
        out_specs=row(D_MODEL),
        out_shape=jax.ShapeDtypeStruct((SEQ, D_MODEL), F32),
        compiler_params=_params(("parallel",)),
        name="out_proj",
    )(x, ya, yb, yc, yd, wo)


def _ffn_kernel(x_ref, g_ref, wg_ref, wu_ref, cg_ref, cu_ref, wd_ref, o_ref, hn, ubuf, carry):
    tm = ROW_TILE
    i = pl.program_id(0)
    j = pl.program_id(1)

    @pl.when(j == 0)
    def _():
        hn[...] = _normed_bf16(x_ref, g_ref)
        o_ref[...] = x_ref[...]

    @pl.when(i == 0)
    def _():
        carry[j] = jnp.zeros(carry.shape[1:], F32)

    def conv(part, w_ref, c_ref):
        u = _dot(hn[...], w_ref[...])
        ubuf[part, 0:SUBLANES, :] = carry[j, part]
        ubuf[part, SUBLANES:, :] = u
        carry[j, part] = u[tm - SUBLANES:, :]
        w = c_ref[...]
        return (w[2:3, :] * u + w[1:2, :] * ubuf[part, SUBLANES - 1:SUBLANES - 1 + tm, :]
                + w[0:1, :] * ubuf[part, SUBLANES - 2:SUBLANES - 2 + tm, :])

    gate = conv(0, wg_ref, cg_ref)
    up = conv(1, wu_ref, cu_ref)
    act = (gate * jax.nn.sigmoid(gate) * up).astype(BF16)
    o_ref[...] += _dot(act, wd_ref[...])


def _ffn(x, g, w_up, w_conv, w_down):
    tm, tf = ROW_TILE, FF_TILE
    nf = D_FF // tf
    return pl.pallas_call(
        _ffn_kernel,
        grid=(SEQ // tm, nf),
        in_specs=[pl.BlockSpec((tm, D_MODEL), lambda i, j: (i, 0)),
                  pl.BlockSpec(g.shape, lambda i, j: (0, 0)),
                  pl.BlockSpec((D_MODEL, tf), lambda i, j: (0, j)),
                  pl.BlockSpec((D_MODEL, tf), lambda i, j: (0, j + nf)),
                  pl.BlockSpec((FFN_K, tf), lambda i, j: (0, j)),
                  pl.BlockSpec((FFN_K, tf), lambda i, j: (0, j + nf)),
                  pl.BlockSpec((tf, D_MODEL), lambda i, j: (j, 0))],
        out_specs=pl.BlockSpec((tm, D_MODEL), lambda i, j: (i, 0)),
        out_shape=jax.ShapeDtypeStruct((SEQ, D_MODEL), F32),
        scratch_shapes=[pltpu.VMEM((tm, D_MODEL), BF16),
                        pltpu.VMEM((2, tm + SUBLANES, tf), F32),
                        pltpu.VMEM((nf, 2, SUBLANES, tf), F32)],
        compiler_params=_params(("arbitrary", "arbitrary")),
        name="ffn",
    )(x, g, w_up, w_up, w_conv, w_conv, w_down)


def _ple_kernel(x_ref, p_ref, g_ref, wpg_ref, wpe_ref, o_ref):
    hn = _normed_bf16(x_ref, g_ref)
    pb = p_ref[...].astype(BF16)
    for n in range(D_MODEL // GROUP):
        cols = slice(n * GROUP, (n + 1) * GROUP)
        gate = jax.nn.sigmoid(_dot(hn, wpg_ref[:, cols]))
        o_ref[:, cols] = x_ref[:, cols] + _dot(pb, wpe_ref[:, cols]) * gate


def _ple(x, p, g, wpg, wpe):
    tm = ROW_TILE
    row = lambda n: pl.BlockSpec((tm, n), lambda i: (i, 0))
    full = lambda a: pl.BlockSpec(a.shape, lambda i: (0, 0))
    return pl.pallas_call(
        _ple_kernel,
        grid=(SEQ // tm,),
        in_specs=[row(D_MODEL), row(PLE_DIM), full(g), full(wpg), full(wpe)],
        out_specs=row(D_MODEL),
        out_shape=jax.ShapeDtypeStruct((SEQ, D_MODEL), F32),
        compiler_params=_params(("parallel",)),
        name="ple",
    )(x, p, g, wpg, wpe)


def _retention_tables():
    c = RET_CHUNK
    lg = jnp.log(1.0 - 2.0 ** (-5.0 - jnp.arange(RET_HEADS, dtype=F32)))
    idx = jnp.arange(c, dtype=F32)
    rel = idx[:, None] - idx[None, :]
    inner = jnp.where(rel[None] >= 0, jnp.exp(lg[:, None, None] * rel[None]), 0.0)
    qdec = jnp.exp(lg[:, None] * (idx + 1.0))[:, :, None]
    kdec_t = jnp.exp(lg[:, None] * (c - 1.0 - idx))
    cdec = jnp.exp(lg * c)
    lane = np.arange(RET_HEADS * RET_DK)
    head_of_lane = (lane % LANES) // (RET_DK // 2)
    hmask = jnp.asarray(head_of_lane[None, :] == np.arange(RET_HEADS)[:, None], F32)
    qmask = hmask[:, None, :]
    kdec = hmask[:, :, None] * kdec_t[:, None, :]
    cdec_b = jnp.broadcast_to(cdec[:, None, None], (RET_HEADS, 1, RET_DV))
    return qmask, kdec, qdec, inner, cdec_b


def _retention_perm():
    half = RET_DK // 2
    new = np.arange(RET_HEADS * RET_DK)
    part, rem = new // LANES, new % LANES
    return (rem // half) * RET_DK + part * half + rem % half


def _lane_freq(inv):
    return jnp.tile(inv, LANES // inv.shape[0])[None, :]


def kernel(x, p, positions, g_mix, w_in, g_qa, g_kva, w_q_up, w_kv_up, g_qn, g_kn, w_sc, w_cf,
           b_cf, g_cf_ln, b_cf_ln, g_ret, w_o, g_ffn, w_up, w_ffn_conv, w_down, g_pe, w_pe, w_pg):
    assert x.shape == (1, SEQ, D_MODEL)
    xs = x[0]
    posf = positions[0].astype(F32)[:, None]
    row = lambda a: a[None, :]

    mla_freq = _lane_freq(ROPE_BASE ** (-jnp.arange(0, MLA_ROPE, 2, dtype=F32) / MLA_ROPE))
    ret_freq = _lane_freq(1.0 / (10000.0 ** jnp.linspace(0.0, 1.0, RET_DK // 2, dtype=F32)))
    qmask, kdec, qdec, inner, cdec = _retention_tables()
    perm = _retention_perm()
    pad_head = lambda a: jnp.pad(a, ((0, 0), (0, HEAD_PAD - MLA_QK)))

    o_mla, o_sc, o_cf, o_ret = np.cumsum([0, Q_LORA + KV_LORA + MLA_ROPE, 3 * GROUP, 2 * GROUP])
    dk2 = RET_HEADS * RET_DK

    for l in range(DEPTH):
        wi = w_in[l]
        wa = jnp.pad(wi[:, :o_sc], ((0, 0), (0, LANES - MLA_ROPE))).astype(BF16)
        wsc_in = wi[:, o_sc:o_cf].astype(BF16)
        wcf_in = wi[:, o_cf:o_ret].astype(BF16)
        wr = wi[:, o_ret:]
        wret_in = jnp.concatenate([wr[:, :dk2][:, perm], wr[:, dk2:2 * dk2][:, perm], wr[:, 2 * dk2:]],
                                  axis=1).astype(BF16)
        wq = jnp.pad(w_q_up[l].reshape(Q_LORA, MLA_HEADS, MLA_QK),
                     ((0, 0), (0, 0), (0, HEAD_PAD - MLA_QK))).reshape(Q_LORA, MLA_HEADS * HEAD_PAD)
        wkv3 = w_kv_up[l].reshape(KV_LORA, MLA_HEADS, MLA_NOPE + MLA_V)
        wkv = jnp.concatenate([wkv3[:, :, :MLA_NOPE].reshape(KV_LORA, -1),
                               wkv3[:, :, MLA_NOPE:].reshape(KV_LORA, -1)], axis=1)
        gq = pad_head(row(g_qn[l]) * (MLA_QK ** -0.5))
        gkn = row(g_kn[l][:MLA_NOPE])
        gkr = jnp.pad(row(g_kn[l][MLA_NOPE:]), ((0, 0), (0, LANES - MLA_ROPE)))
        gm = row(g_mix[l])

        q, k, v = _mla_pre(xs, posf, gm, wa, row(g_qa[l]), row(g_kva[l]), wq.astype(BF16),
                           wkv.astype(BF16), gq, gkn, gkr, mla_freq)
        y_a = _flash(q, k, v)
        y_b = _sconv(xs, gm, wsc_in, w_sc[l])
        y_c = _conformer(xs, gm, wcf_in, w_cf[l], row(b_cf[l]), row(g_cf_ln[l]), row(b_cf_ln[l]))
        y_d = _retention(xs, posf, gm, wret_in, ret_freq, qmask, kdec, qdec, inner, cdec,
                         g_ret[l][:, None, :])
        xs = _out_proj(xs, y_a, y_b, y_c, y_d, w_o[l].astype(BF16))
        xs = _ffn(xs, row(g_ffn[l]), w_up[l].astype(BF16), w_ffn_conv[l], w_down[l].astype(BF16))
        xs = _ple(xs, p[l, 0], row(g_pe[l]), w_pg[l].astype(BF16), w_pe[l].astype(BF16))
    return xs[None]
```

```python
import functools

import numpy as np
import jax
import jax.numpy as jnp
from jax import lax
from jax.experimental import pallas as pl
from jax.experimental.pallas import tpu as pltpu

F32 = jnp.float32
BF16 = jnp.bfloat16

D_MODEL = 2048
SEQ = 8192
DEPTH = 2
GROUP = 512
MLA_HEADS = 4
MLA_NOPE = 128
MLA_ROPE = 64
MLA_V = 128
MLA_QK = MLA_NOPE + MLA_ROPE
Q_LORA = 512
KV_LORA = 256
ROPE_BASE = 10000.0
SC_K = 3
CF_K = 31
RET_HEADS = 4
RET_DV = 128
RET_DK = 64
RET_CHUNK = 128
D_FF = 5632
FFN_K = 3
PLE_DIM = 256
EPS = 1e-6

LANES = 128
SUBLANES = 8
HEAD_PAD = 2 * LANES
ROW_TILE = 512
FF_TILE = 512
ATT_TILE = 512
ATT_HEADS_PER_STEP = 2
CF_HALO = 32
VMEM_LIMIT = 56 * 1024 * 1024
MASK_VALUE = -1e30
LOG2_E = 1.4426950408889634


def _params(sem):
    return pltpu.CompilerParams(dimension_semantics=sem, vmem_limit_bytes=VMEM_LIMIT)


def _rms_scale(xf, n):
    return lax.rsqrt(jnp.sum(xf * xf, axis=-1, keepdims=True) * (1.0 / n) + EPS)


def _normed_bf16(x_ref, g_ref):
    xf = x_ref[...]
    return (xf * _rms_scale(xf, D_MODEL) * g_ref[...]).astype(BF16)


def _dot(a, b):
    return jnp.dot(a, b, preferred_element_type=F32)


def _rope_slab(x, cos_t, sin_a, sin_b):
    return (x * cos_t + pltpu.roll(x, LANES - 32, axis=1) * sin_a
            + pltpu.roll(x, 32, axis=1) * sin_b)


def _mla_pre_kernel(x_ref, pos_ref, gmix_ref, wa_ref, gqa_ref, gkva_ref, wq_ref, wkv_ref,
                    gq_ref, gkn_ref, gkr_ref, freq_ref, q_out, k_out, v_out):
    h = _normed_bf16(x_ref, gmix_ref)
    z = _dot(h, wa_ref[...])
    zq = z[:, :Q_LORA]
    zkv = z[:, Q_LORA:Q_LORA + KV_LORA]
    zkr = z[:, Q_LORA + KV_LORA:]
    c_q = (zq * _rms_scale(zq, Q_LORA) * gqa_ref[...]).astype(BF16)
    c_kv = (zkv * _rms_scale(zkv, KV_LORA) * gkva_ref[...]).astype(BF16)

    ang = pos_ref[...] * freq_ref[...]
    lane = lax.broadcasted_iota(jnp.int32, ang.shape, 1)
    cos_t = jnp.cos(ang)
    sin_t = jnp.sin(ang)
    sin_a = jnp.where(lane < 32, -sin_t, 0.0)
    sin_b = jnp.where((lane >= 32) & (lane < 64), sin_t, 0.0)

    q = _dot(c_q, wq_ref[...])
    kv = _dot(c_kv, wkv_ref[...])
    v_out[...] = kv[:, MLA_HEADS * MLA_NOPE:].astype(BF16)

    kr = zkr * gkr_ref[...]
    ss_kr = jnp.sum(zkr * zkr, axis=-1, keepdims=True)
    kr_rot = _rope_slab(kr, cos_t, sin_a, sin_b)
    for hd in range(MLA_HEADS):
        qb = q[:, hd * HEAD_PAD:(hd + 1) * HEAD_PAD]
        qn = qb * _rms_scale(qb, MLA_QK) * gq_ref[...]
        q_out[:, hd * HEAD_PAD:hd * HEAD_PAD + LANES] = qn[:, :LANES].astype(BF16)
        q_out[:, hd * HEAD_PAD + LANES:(hd + 1) * HEAD_PAD] = _rope_slab(
            qn[:, LANES:], cos_t, sin_a, sin_b).astype(BF16)
        kn = kv[:, hd * MLA_NOPE:(hd + 1) * MLA_NOPE]
        r_k = lax.rsqrt((jnp.sum(kn * kn, axis=-1, keepdims=True) + ss_kr) * (1.0 / MLA_QK) + EPS)
        k_out[:, hd * HEAD_PAD:hd * HEAD_PAD + LANES] = (kn * r_k * gkn_ref[...]).astype(BF16)
        k_out[:, hd * HEAD_PAD + LANES:(hd + 1) * HEAD_PAD] = (kr_rot * r_k).astype(BF16)


def _mla_pre(x, posf, g_mix, wa, g_qa, g_kva, wq, wkv, gq, gkn, gkr, freq):
    tm = ROW_TILE
    row = lambda n: pl.BlockSpec((tm, n), lambda i: (i, 0))
    full = lambda a: pl.BlockSpec(a.shape, lambda i: (0, 0))
    return pl.pallas_call(
        _mla_pre_kernel,
        grid=(SEQ // tm,),
        in_specs=[row(D_MODEL), row(1)] + [full(a) for a in
                                           (g_mix, wa, g_qa, g_kva, wq, wkv, gq, gkn, gkr, freq)],
        out_specs=[row(MLA_HEADS * HEAD_PAD), row(MLA_HEADS * HEAD_PAD), row(MLA_HEADS * MLA_V)],
        out_shape=[jax.ShapeDtypeStruct((SEQ, MLA_HEADS * HEAD_PAD), BF16),
                   jax.ShapeDtypeStruct((SEQ, MLA_HEADS * HEAD_PAD), BF16),
                   jax.ShapeDtypeStruct((SEQ, MLA_HEADS * MLA_V), BF16)],
        compiler_params=_params(("parallel",)),
        name="mla_pre",
    )(x, posf, g_mix, wa, g_qa, g_kva, wq, wkv, gq, gkn, gkr, freq)


def _flash_kernel(q_ref, k_ref, v_ref, o_ref, m_scr, acc_scr):
    t = ATT_TILE
    i = pl.program_id(1)
    m_scr[...] = jnp.full(m_scr.shape, MASK_VALUE, F32)
    acc_scr[...] = jnp.zeros(acc_scr.shape, F32)
    ones = jnp.ones((t, LANES), BF16)

    def block(j, masked):
        start = pl.multiple_of(j * t, t)
        if masked:
            r = lax.broadcasted_iota(jnp.int32, (t, t), 0)
            c = lax.broadcasted_iota(jnp.int32, (t, t), 1)
            keep = c <= r
        for hh in range(ATT_HEADS_PER_STEP):
            q = q_ref[:, hh * HEAD_PAD:(hh + 1) * HEAD_PAD]
            k = k_ref[pl.ds(start, t), hh * HEAD_PAD:(hh + 1) * HEAD_PAD]
            v = v_ref[pl.ds(start, t), hh * MLA_V:(hh + 1) * MLA_V]
            s = lax.dot_general(q, k, (((1,), (1,)), ((), ())), preferred_element_type=F32)
            if masked:
                s = jnp.where(keep, s, MASK_VALUE)
            slabs = [s[:, n * LANES:(n + 1) * LANES] for n in range(t // LANES)]
            m_prev = m_scr[hh]
            m_new = jnp.maximum(m_prev, jnp.max(functools.reduce(jnp.maximum, slabs),
                                                axis=-1, keepdims=True))
            alpha = jnp.exp2(m_prev - m_new)
            p = jnp.concatenate([jnp.exp2((sl - m_new).astype(BF16)) for sl in slabs], axis=1)
            pv = _dot(p, jnp.concatenate([v, ones], axis=1))
            acc_scr[hh, :, :MLA_V] = alpha * acc_scr[hh, :, :MLA_V] + pv[:, :MLA_V]
            acc_scr[hh, :, MLA_V:] = alpha * acc_scr[hh, :, MLA_V:] + pv[:, MLA_V:]
            m_scr[hh] = m_new

    def body(j, carry):
        block(j, False)
        return carry

    lax.fori_loop(0, i, body, 0)
    block(i, True)
    for hh in range(ATT_HEADS_PER_STEP):
        o_ref[:, hh * MLA_V:(hh + 1) * MLA_V] = (
            acc_scr[hh, :, :MLA_V] / acc_scr[hh, :, MLA_V:]).astype(o_ref.dtype)


def _flash(q, k, v):
    t, hp = ATT_TILE, ATT_HEADS_PER_STEP
    return pl.pallas_call(
        _flash_kernel,
        grid=(MLA_HEADS // hp, SEQ // t),
        in_specs=[pl.BlockSpec((t, hp * HEAD_PAD), lambda h, i: (i, h)),
                  pl.BlockSpec((SEQ, hp * HEAD_PAD), lambda h, i: (0, h)),
                  pl.BlockSpec((SEQ, hp * MLA_V), lambda h, i: (0, h))],
        out_specs=pl.BlockSpec((t, hp * MLA_V), lambda h, i: (i, h)),
        out_shape=jax.ShapeDtypeStruct((SEQ, MLA_HEADS * MLA_V), BF16),
        scratch_shapes=[pltpu.VMEM((hp, t, LANES), F32), pltpu.VMEM((hp, t, 2 * MLA_V), F32)],
        compiler_params=_params(("parallel", "arbitrary")),
        name="flash",
    )(q, k, v)


def _sconv_kernel(x_ref, gmix_ref, w_ref, wsc_ref, o_ref, buf):
    tm = ROW_TILE
    h = _normed_bf16(x_ref, gmix_ref)
    z = _dot(h, w_ref[...])

    @pl.when(pl.program_id(0) == 0)
    def _():
        buf[0:SUBLANES, :] = jnp.zeros((SUBLANES, GROUP), F32)

    g = z[:, GROUP:2 * GROUP] * z[:, 2 * GROUP:]
    buf[SUBLANES:, :] = g
    w = wsc_ref[...]
    conv = (w[2:3, :] * g + w[1:2, :] * buf[SUBLANES - 1:SUBLANES - 1 + tm, :]
            + w[0:1, :] * buf[SUBLANES - 2:SUBLANES - 2 + tm, :])
    o_ref[...] = (z[:, :GROUP] * conv).astype(o_ref.dtype)
    buf[0:SUBLANES, :] = buf[tm:tm + SUBLANES, :]


def _sconv(x, g_mix, w, wsc):
    tm = ROW_TILE
    return pl.pallas_call(
        _sconv_kernel,
        grid=(SEQ // tm,),
        in_specs=[pl.BlockSpec((tm, D_MODEL), lambda i: (i, 0)),
                  pl.BlockSpec(g_mix.shape, lambda i: (0, 0)),
                  pl.BlockSpec(w.shape, lambda i: (0, 0)),
                  pl.BlockSpec(wsc.shape, lambda i: (0, 0))],
        out_specs=pl.BlockSpec((tm, GROUP), lambda i: (i, 0)),
        out_shape=jax.ShapeDtypeStruct((SEQ, GROUP), BF16),
        scratch_shapes=[pltpu.VMEM((tm + SUBLANES, GROUP), F32)],
        compiler_params=_params(("arbitrary",)),
        name="sconv",
    )(x, g_mix, w, wsc)


def _conformer_kernel(x_ref, gmix_ref, w_ref, wcf_ref, bcf_ref, gln_ref, bln_ref, o_ref, buf):
    tm = ROW_TILE
    h = _normed_bf16(x_ref, gmix_ref)
    z = _dot(h, w_ref[...])

    @pl.when(pl.program_id(0) == 0)
    def _():
        buf[0:CF_HALO, :] = jnp.zeros((CF_HALO, GROUP), F32)

    buf[CF_HALO:, :] = z[:, :GROUP] * jax.nn.sigmoid(z[:, GROUP:])
    w = wcf_ref[...]
    base = CF_HALO - (CF_K - 1)
    u = bcf_ref[...] + w[0:1, :] * buf[base:base + tm, :]
    for j in range(1, CF_K):
        u = u + w[j:j + 1, :] * buf[base + j:base + j + tm, :]
    mu = jnp.mean(u, axis=-1, keepdims=True)
    uc = u - mu
    y = uc * lax.rsqrt(jnp.mean(uc * uc, axis=-1, keepdims=True) + EPS)
    y = y * gln_ref[...] + bln_ref[...]
    o_ref[...] = (y * jax.nn.sigmoid(y)).astype(o_ref.dtype)
    buf[0:CF_HALO, :] = buf[tm:tm + CF_HALO, :]


def _conformer(x, g_mix, w, wcf, bcf, gln, bln):
    tm = ROW_TILE
    full = lambda a: pl.BlockSpec(a.shape, lambda i: (0, 0))
    return pl.pallas_call(
        _conformer_kernel,
        grid=(SEQ // tm,),
        in_specs=[pl.BlockSpec((tm, D_MODEL), lambda i: (i, 0))]
        + [full(a) for a in (g_mix, w, wcf, bcf, gln, bln)],
        out_specs=pl.BlockSpec((tm, GROUP), lambda i: (i, 0)),
        out_shape=jax.ShapeDtypeStruct((SEQ, GROUP), BF16),
        scratch_shapes=[pltpu.VMEM((tm + CF_HALO, GROUP), F32)],
        compiler_params=_params(("arbitrary",)),
        name="conformer",
    )(x, g_mix, w, wcf, bcf, gln, bln)


def _retention_kernel(x_ref, pos_ref, gmix_ref, w_ref, freq_ref, qmask_ref, kdec_ref, qdec_ref,
                      inner_ref, cdec_ref, gret_ref, o_ref, state):
    tm = ROW_TILE
    c = RET_CHUNK
    dk2 = RET_HEADS * RET_DK
    h = _normed_bf16(x_ref, gmix_ref)
    z = _dot(h, w_ref[...])

    @pl.when(pl.program_id(0) == 0)
    def _():
        state[...] = jnp.zeros(state.shape, F32)

    ang = pos_ref[...] * freq_ref[...]
    cos_t = jnp.cos(ang)
    sin_t = jnp.sin(ang)

    def rot(t):
        t1, t2 = t[:, :LANES], t[:, LANES:]
        return jnp.concatenate([t1 * cos_t - t2 * sin_t, t2 * cos_t + t1 * sin_t], axis=1)

    q = rot(z[:, :dk2])
    k = rot(z[:, dk2:2 * dk2]) * (RET_DK ** -0.5)
    v = z[:, 2 * dk2:2 * dk2 + GROUP]
    gate = z[:, 2 * dk2 + GROUP:]

    for ci in range(tm // c):
        rows = slice(ci * c, (ci + 1) * c)
        qc = q[rows]
        kt = k[rows].T
        kt_b = kt.astype(BF16)
        for hd in range(RET_HEADS):
            cols = slice(hd * RET_DV, (hd + 1) * RET_DV)
            qm = (qc * qmask_ref[hd]).astype(BF16)
            vh = v[rows, cols].astype(BF16)
            sc = _dot(qm, kt_b) * inner_ref[hd]
            st = state[hd]
            o = _dot(sc.astype(BF16), vh) + _dot(qm, st.astype(BF16)) * qdec_ref[hd]
            state[hd] = st * cdec_ref[hd] + _dot((kt * kdec_ref[hd]).astype(BF16), vh)
            on = o * _rms_scale(o, RET_DV) * gret_ref[hd]
            gh = gate[rows, cols]
            o_ref[rows, cols] = (gh * jax.nn.sigmoid(gh) * on).astype(o_ref.dtype)


def _retention(x, posf, g_mix, w, freq, qmask, kdec, qdec, inner, cdec, gret):
    tm = ROW_TILE
    full = lambda a: pl.BlockSpec(a.shape, lambda i: (0,) * a.ndim)
    return pl.pallas_call(
        _retention_kernel,
        grid=(SEQ // tm,),
        in_specs=[pl.BlockSpec((tm, D_MODEL), lambda i: (i, 0)),
                  pl.BlockSpec((tm, 1), lambda i: (i, 0))]
        + [full(a) for a in (g_mix, w, freq, qmask, kdec, qdec, inner, cdec, gret)],
        out_specs=pl.BlockSpec((tm, GROUP), lambda i: (i, 0)),
        out_shape=jax.ShapeDtypeStruct((SEQ, GROUP), BF16),
        scratch_shapes=[pltpu.VMEM((RET_HEADS, RET_HEADS * RET_DK, RET_DV), F32)],
        compiler_params=_params(("arbitrary",)),
        name="retention",
    )(x, posf, g_mix, w, freq, qmask, kdec, qdec, inner, cdec, gret)


def _out_proj_kernel(x_ref, ya_ref, yb_ref, yc_ref, yd_ref, wo_ref, o_ref):
    acc = x_ref[...]
    for n, y_ref in enumerate((ya_ref, yb_ref, yc_ref, yd_ref)):
        acc = acc + _dot(y_ref[...], wo_ref[n * GROUP:(n + 1) * GROUP, :])
    o_ref[...] = acc


def _out_proj(x, ya, yb, yc, yd, wo):
    tm = ROW_TILE
    row = lambda n: pl.BlockSpec((tm, n), lambda i: (i, 0))
    return pl.pallas_call(
        _out_proj_kernel,
        grid=(SEQ // tm,),
        in_specs=[row(D_MODEL), row(GROUP), row(GROUP), row(GROUP), row(GROUP),
                  pl.BlockSpec(wo.shape, lambda i: (0, 0))],
        out_specs=row(D_MODEL),
        out_shape=jax.ShapeDtypeStruct((SEQ, D_MODEL), F32),
        compiler_params=_params(("parallel",)),
        name="out_proj",
    )(x, ya, yb, yc, yd, wo)


def _ffn_kernel(x_ref, g_ref, wg_ref, wu_ref, cg_ref, cu_ref, wd_ref, o_ref, hn, ubuf, carry):
    tm = ROW_TILE
    i = pl.program_id(0)
    j = pl.program_id(1)

    @pl.when(j == 0)
    def _():
        hn[...] = _normed_bf16(x_ref, g_ref)
        o_ref[...] = x_ref[...]

    @pl.when(i == 0)
    def _():
        carry[j] = jnp.zeros(carry.shape[1:], F32)

    def conv(part, w_ref, c_ref):
        u = _dot(hn[...], w_ref[...])
        ubuf[part, 0:SUBLANES, :] = carry[j, part]
        ubuf[part, SUBLANES:, :] = u
        carry[j, part] = u[tm - SUBLANES:, :]
        w = c_ref[...]
        return (w[2:3, :] * u + w[1:2, :] * ubuf[part, SUBLANES - 1:SUBLANES - 1 + tm, :]
                + w[0:1, :] * ubuf[part, SUBLANES - 2:SUBLANES - 2 + tm, :])

    gate = conv(0, wg_ref, cg_ref)
    up = conv(1, wu_ref, cu_ref)
    act = (gate * jax.nn.sigmoid(gate) * up).astype(BF16)
    o_ref[...] += _dot(act, wd_ref[...])


def _ffn(x, g, w_up, w_conv, w_down):
    tm, tf = ROW_TILE, FF_TILE
    nf = D_FF // tf
    return pl.pallas_call(
        _ffn_kernel,
        grid=(SEQ // tm, nf),
        in_specs=[pl.BlockSpec((tm, D_MODEL), lambda i, j: (i, 0)),
                  pl.BlockSpec(g.shape, lambda i, j: (0, 0)),
                  pl.BlockSpec((D_MODEL, tf), lambda i, j: (0, j)),
                  pl.BlockSpec((D_MODEL, tf), lambda i, j: (0, j + nf)),
                  pl.BlockSpec((FFN_K, tf), lambda i, j: (0, j)),
                  pl.BlockSpec((FFN_K, tf), lambda i, j: (0, j + nf)),
                  pl.BlockSpec((tf, D_MODEL), lambda i, j: (j, 0))],
        out_specs=pl.BlockSpec((tm, D_MODEL), lambda i, j: (i, 0)),
        out_shape=jax.ShapeDtypeStruct((SEQ, D_MODEL), F32),
        scratch_shapes=[pltpu.VMEM((tm, D_MODEL), BF16),
                        pltpu.VMEM((2, tm + SUBLANES, tf), F32),
                        pltpu.VMEM((nf, 2, SUBLANES, tf), F32)],
        compiler_params=_params(("arbitrary", "arbitrary")),
        name="ffn",
    )(x, g, w_up, w_up, w_conv, w_conv, w_down)


def _ple_kernel(x_ref, p_ref, g_ref, wpg_ref, wpe_ref, o_ref):
    hn = _normed_bf16(x_ref, g_ref)
    pb = p_ref[...].astype(BF16)
    for n in range(D_MODEL // GROUP):
        cols = slice(n * GROUP, (n + 1) * GROUP)
        gate = jax.nn.sigmoid(_dot(hn, wpg_ref[:, cols]))
        o_ref[:, cols] = x_ref[:, cols] + _dot(pb, wpe_ref[:, cols]) * gate


def _ple(x, p, g, wpg, wpe):
    tm = ROW_TILE
    row = lambda n: pl.BlockSpec((tm, n), lambda i: (i, 0))
    full = lambda a: pl.BlockSpec(a.shape, lambda i: (0, 0))
    return pl.pallas_call(
        _ple_kernel,
        grid=(SEQ // tm,),
        in_specs=[row(D_MODEL), row(PLE_DIM), full(g), full(wpg), full(wpe)],
        out_specs=row(D_MODEL),
        out_shape=jax.ShapeDtypeStruct((SEQ, D_MODEL), F32),
        compiler_params=_params(("parallel",)),
        name="ple",
    )(x, p, g, wpg, wpe)


def _retention_tables():
    c = RET_CHUNK
    lg = jnp.log(1.0 - 2.0 ** (-5.0 - jnp.arange(RET_HEADS, dtype=F32)))
    idx = jnp.arange(c, dtype=F32)
    rel = idx[:, None] - idx[None, :]
    inner = jnp.where(rel[None] >= 0, jnp.exp(lg[:, None, None] * rel[None]), 0.0)
    qdec = jnp.exp(lg[:, None] * (idx + 1.0))[:, :, None]
    kdec_t = jnp.exp(lg[:, None] * (c - 1.0 - idx))
    cdec = jnp.exp(lg * c)
    lane = np.arange(RET_HEADS * RET_DK)
    head_of_lane = (lane % LANES) // (RET_DK // 2)
    hmask = jnp.asarray(head_of_lane[None, :] == np.arange(RET_HEADS)[:, None], F32)
    qmask = hmask[:, None, :]
    kdec = hmask[:, :, None] * kdec_t[:, None, :]
    cdec_b = jnp.broadcast_to(cdec[:, None, None], (RET_HEADS, 1, RET_DV))
    return qmask, kdec, qdec, inner, cdec_b


def _retention_perm():
    half = RET_DK // 2
    new = np.arange(RET_HEADS * RET_DK)
    part, rem = new // LANES, new % LANES
    return (rem // half) * RET_DK + part * half + rem % half


def _lane_freq(inv):
    return jnp.tile(inv, LANES // inv.shape[0])[None, :]


def kernel(x, p, positions, g_mix, w_in, g_qa, g_kva, w_q_up, w_kv_up, g_qn, g_kn, w_sc, w_cf,
           b_cf, g_cf_ln, b_cf_ln, g_ret, w_o, g_ffn, w_up, w_ffn_conv, w_down, g_pe, w_pe, w_pg):
    assert x.shape == (1, SEQ, D_MODEL)
    xs = x[0]
    posf = positions[0].astype(F32)[:, None]
    row = lambda a: a[None, :]

    mla_freq = _lane_freq(ROPE_BASE ** (-jnp.arange(0, MLA_ROPE, 2, dtype=F32) / MLA_ROPE))
    ret_freq = _lane_freq(1.0 / (10000.0 ** jnp.linspace(0.0, 1.0, RET_DK // 2, dtype=F32)))
    qmask, kdec, qdec, inner, cdec = _retention_tables()
    perm = _retention_perm()
    pad_head = lambda a: jnp.pad(a, ((0, 0), (0, HEAD_PAD - MLA_QK)))

    o_mla, o_sc, o_cf, o_ret = np.cumsum([0, Q_LORA + KV_LORA + MLA_ROPE, 3 * GROUP, 2 * GROUP])
    dk2 = RET_HEADS * RET_DK

    for l in range(DEPTH):
        wi = w_in[l]
        wa = jnp.pad(wi[:, :o_sc], ((0, 0), (0, LANES - MLA_ROPE))).astype(BF16)
        wsc_in = wi[:, o_sc:o_cf].astype(BF16)
        wcf_in = wi[:, o_cf:o_ret].astype(BF16)
        wr = wi[:, o_ret:]
        wret_in = jnp.concatenate([wr[:, :dk2][:, perm], wr[:, dk2:2 * dk2][:, perm], wr[:, 2 * dk2:]],
                                  axis=1).astype(BF16)
        wq = jnp.pad(w_q_up[l].reshape(Q_LORA, MLA_HEADS, MLA_QK),
                     ((0, 0), (0, 0), (0, HEAD_PAD - MLA_QK))).reshape(Q_LORA, MLA_HEADS * HEAD_PAD)
        wkv3 = w_kv_up[l].reshape(KV_LORA, MLA_HEADS, MLA_NOPE + MLA_V)
        wkv = jnp.concatenate([wkv3[:, :, :MLA_NOPE].reshape(KV_LORA, -1),
                               wkv3[:, :, MLA_NOPE:].reshape(KV_LORA, -1)], axis=1)
        gq = pad_head(row(g_qn[l]) * (MLA_QK ** -0.5 * LOG2_E))
        gkn = row(g_kn[l][:MLA_NOPE])
        gkr = jnp.pad(row(g_kn[l][MLA_NOPE:]), ((0, 0), (0, LANES - MLA_ROPE)))
        gm = row(g_mix[l])

        q, k, v = _mla_pre(xs, posf, gm, wa, row(g_qa[l]), row(g_kva[l]), wq.astype(BF16),
                           wkv.astype(BF16), gq, gkn, gkr, mla_freq)
        y_a = _flash(q, k, v)
        y_b = _sconv(xs, gm, wsc_in, w_sc[l])
        y_c = _conformer(xs, gm, wcf_in, w_cf[l], row(b_cf[l]), row(g_cf_ln[l]), row(b_cf_ln[l]))
        y_d = _retention(xs, posf, gm, wret_in, ret_freq, qmask, kdec, qdec, inner, cdec,
                         g_ret[l][:, None, :])
        xs = _out_proj(xs, y_a, y_b, y_c, y_d, w_o[l].astype(BF16))
        xs = _ffn(xs, row(g_ffn[l]), w_up[l].astype(BF16), w_ffn_conv[l], w_down[l].astype(BF16))
        xs = _ple(xs, p[l, 0], row(g_pe[l]), w_pg[l].astype(BF16), w_pe[l].astype(BF16))
    return xs[None]
```

```python
import numpy as np
import jax
import jax.numpy as jnp
from jax import lax
from jax.experimental import pallas as pl
from jax.experimental.pallas import tpu as pltpu

F32 = jnp.float32
BF16 = jnp.bfloat16

D_MODEL = 2048
SEQ = 8192
DEPTH = 2
GROUP = 512
MLA_HEADS = 4
MLA_NOPE = 128
MLA_ROPE = 64
MLA_V = 128
MLA_QK = MLA_NOPE + MLA_ROPE
Q_LORA = 512
KV_LORA = 256
ROPE_BASE = 10000.0
SC_K = 3
CF_K = 31
RET_HEADS = 4
RET_DV = 128
RET_DK = 64
RET_CHUNK = 128
D_FF = 5632
FFN_K = 3
PLE_DIM = 256
EPS = 1e-6
N_IN = Q_LORA + KV_LORA + MLA_ROPE + 3 * GROUP + 2 * GROUP + 2 * RET_HEADS * (RET_DK + RET_DV)

LANES = 128
SUBLANES = 8
HEAD_PAD = 2 * LANES
ROW_TILE = 512
FF_TILE = 512
FF_SUB = 2
DOT_ROWS = 128
ATT_TILE = 512
ATT_HEADS_PER_STEP = 4
V_ONES = 16
CF_HALO = 32
CF_ROWS = 128
MLA_IN = Q_LORA + KV_LORA + LANES
W_IN_GROUPS = {"sconv": (3 * GROUP, 0), "retention": (3 * GROUP, 1), "conformer": (2 * GROUP, 3),
               "mla": (2 * GROUP, 4)}
W_IN_COLS = 5 * 2 * GROUP
VMEM_LIMIT = 56 * 1024 * 1024
MASK_VALUE = -1e30
LOG2_E = 1.4426950408889634


def _lay(a, l):
    return pl.BlockSpec((None,) + a.shape[1:], lambda *_: (l,) + (0,) * (a.ndim - 1))


def _const(a):
    return pl.BlockSpec(a.shape, lambda *_: (0,) * a.ndim)


def _w_in_spec(l, group):
    width, block = W_IN_GROUPS[group]
    return pl.BlockSpec((None, D_MODEL, width), lambda *_: (l, 0, block))


def _params(sem):
    return pltpu.CompilerParams(dimension_semantics=sem, vmem_limit_bytes=VMEM_LIMIT)


def _rms_scale(xf, n):
    return lax.rsqrt(jnp.sum(xf * xf, axis=-1, keepdims=True) * (1.0 / n) + EPS)


def _normed_bf16(x_ref, g_ref):
    xf = x_ref[...]
    return (xf * _rms_scale(xf, D_MODEL) * g_ref[...]).astype(BF16)


def _dot(a, b):
    return jnp.dot(a, b, preferred_element_type=F32)


def _rdot(a, b):
    n = a.shape[0] // DOT_ROWS
    return jnp.concatenate([_dot(a[r * DOT_ROWS:(r + 1) * DOT_ROWS], b) for r in range(n)], axis=0)


def _rope_slab(x, cos_t, sin_a, sin_b):
    return (x * cos_t + pltpu.roll(x, LANES - 32, axis=1) * sin_a
            + pltpu.roll(x, 32, axis=1) * sin_b)


def _mla_pre_kernel(x_ref, pos_ref, gmix_ref, wa_ref, gqa_ref, gkva_ref, wq_ref, wkv_ref,
                    gq_ref, gkn_ref, gkr_ref, freq_ref, qt_out, k_out, vt_out):
    h = _normed_bf16(x_ref, gmix_ref)
    z = _rdot(h, wa_ref[:, :MLA_IN])
    zq = z[:, :Q_LORA]
    zkv = z[:, Q_LORA:Q_LORA + KV_LORA]
    zkr = z[:, Q_LORA + KV_LORA:]
    c_q = (zq * _rms_scale(zq, Q_LORA) * gqa_ref[...]).astype(BF16)
    c_kv = (zkv * _rms_scale(zkv, KV_LORA) * gkva_ref[...]).astype(BF16)

    ang = pos_ref[...] * freq_ref[...]
    lane = lax.broadcasted_iota(jnp.int32, ang.shape, 1)
    cos_t = jnp.cos(ang)
    sin_t = jnp.sin(ang)
    sin_a = jnp.where(lane < 32, -sin_t, 0.0)
    sin_b = jnp.where((lane >= 32) & (lane < 64), sin_t, 0.0)

    q = _rdot(c_q, wq_ref[...])
    kv = _rdot(c_kv, wkv_ref[...])

    kr = zkr * gkr_ref[...]
    ss_kr = jnp.sum(zkr * zkr, axis=-1, keepdims=True)
    kr_rot = _rope_slab(kr, cos_t, sin_a, sin_b)
    ones = jnp.ones((V_ONES, z.shape[0]), BF16)
    for hd in range(MLA_HEADS):
        qb = q[:, hd * HEAD_PAD:(hd + 1) * HEAD_PAD]
        qn = qb * _rms_scale(qb, MLA_QK) * gq_ref[...]
        qt_out[hd, 0:LANES, :] = qn[:, :LANES].T.astype(BF16)
        qt_out[hd, LANES:, :] = _rope_slab(qn[:, LANES:], cos_t, sin_a, sin_b).T.astype(BF16)
        vh = kv[:, MLA_HEADS * MLA_NOPE + hd * MLA_V:MLA_HEADS * MLA_NOPE + (hd + 1) * MLA_V]
        vt_out[hd, 0, 0:MLA_V, :] = vh.T.astype(BF16)
        vt_out[hd, 0, MLA_V:, :] = ones
        kn = kv[:, hd * MLA_NOPE:(hd + 1) * MLA_NOPE]
        r_k = lax.rsqrt((jnp.sum(kn * kn, axis=-1, keepdims=True) + ss_kr) * (1.0 / MLA_QK) + EPS)
        k_out[:, hd * HEAD_PAD:hd * HEAD_PAD + LANES] = (kn * r_k * gkn_ref[...]).astype(BF16)
        k_out[:, hd * HEAD_PAD + LANES:(hd + 1) * HEAD_PAD] = (kr_rot * r_k).astype(BF16)


def _mla_pre(l, x, posf, g_mix, w_in, g_qa, g_kva, wq, wkv, gq, gkn, gkr, freq):
    tm = ROW_TILE
    row = lambda n: pl.BlockSpec((tm, n), lambda i: (i, 0))
    return pl.pallas_call(
        _mla_pre_kernel,
        grid=(SEQ // tm,),
        in_specs=[row(D_MODEL), row(1), _lay(g_mix, l), _w_in_spec(l, "mla")]
        + [_lay(a, l) for a in (g_qa, g_kva, wq, wkv, gq, gkn, gkr)] + [_const(freq)],
        out_specs=[pl.BlockSpec((MLA_HEADS, HEAD_PAD, tm), lambda i: (0, 0, i)),
                   row(MLA_HEADS * HEAD_PAD),
                   pl.BlockSpec((MLA_HEADS, 1, MLA_V + V_ONES, tm), lambda i: (0, i, 0, 0))],
        out_shape=[jax.ShapeDtypeStruct((MLA_HEADS, HEAD_PAD, SEQ), BF16),
                   jax.ShapeDtypeStruct((SEQ, MLA_HEADS * HEAD_PAD), BF16),
                   jax.ShapeDtypeStruct((MLA_HEADS, SEQ // tm, MLA_V + V_ONES, tm), BF16)],
        compiler_params=_params(("parallel",)),
        name="mla_pre",
    )(x, posf, g_mix, w_in, g_qa, g_kva, wq, wkv, gq, gkn, gkr, freq)


def _flash_kernel(qt_ref, k_ref, vt_ref, o_ref, m_scr, acc_scr):
    t = ATT_TILE
    i = pl.program_id(1)
    sub = SUBLANES
    pack = 2 * SUBLANES
    vrows = MLA_V + V_ONES
    m_scr[...] = jnp.full(m_scr.shape, MASK_VALUE, F32)
    acc_scr[...] = jnp.zeros(acc_scr.shape, F32)

    def block(j, masked):
        start = pl.multiple_of(j * t, t)
        if masked:
            key = lax.broadcasted_iota(jnp.int32, (t, t), 0)
            qry = lax.broadcasted_iota(jnp.int32, (t, t), 1)
            keep = key <= qry
        sts = [_rdot(k_ref[pl.ds(start, t), hh * HEAD_PAD:(hh + 1) * HEAD_PAD], qt_ref[hh])
               for hh in range(ATT_HEADS_PER_STEP)]
        for hh, st in enumerate(sts):
            if masked:
                st = jnp.where(keep, st, MASK_VALUE)
            s3 = st.reshape(t // pack, pack, t)
            mx = jnp.max(s3, axis=0)
            mx = jnp.maximum(mx[:sub], mx[sub:])
            for shift in (4, 2, 1):
                mx = jnp.maximum(mx, pltpu.roll(mx, shift, axis=0))
            m_prev = m_scr[hh]
            m_new = jnp.maximum(m_prev, mx)
            alpha = jnp.exp2(m_prev - m_new)
            m2 = jnp.concatenate([m_new, m_new], axis=0)
            p = jnp.exp2((s3 - m2[None]).astype(BF16)).reshape(t, t)
            pv = _dot(vt_ref[hh, j], p)
            acc = acc_scr[hh].reshape(vrows // sub, sub, t) * alpha[None] + pv.reshape(vrows // sub, sub, t)
            acc_scr[hh] = acc.reshape(vrows, t)
            m_scr[hh] = m_new

    def body(j, carry):
        block(j, False)
        return carry

    lax.fori_loop(0, i, body, 0)
    block(i, True)
    for hh in range(ATT_HEADS_PER_STEP):
        num = acc_scr[hh, 0:MLA_V, :].reshape(MLA_V // sub, sub, t)
        den = acc_scr[hh, MLA_V:MLA_V + sub, :]
        out_t = (num / den[None]).reshape(MLA_V, t)
        o_ref[:, hh * MLA_V:(hh + 1) * MLA_V] = out_t.T.astype(o_ref.dtype)


def _flash(qt, k, vt):
    t, hp = ATT_TILE, ATT_HEADS_PER_STEP
    vrows = MLA_V + V_ONES
    return pl.pallas_call(
        _flash_kernel,
        grid=(MLA_HEADS // hp, SEQ // t),
        in_specs=[pl.BlockSpec((hp, HEAD_PAD, t), lambda h, i: (h, 0, i)),
                  pl.BlockSpec((SEQ, hp * HEAD_PAD), lambda h, i: (0, h), pipeline_mode=pl.Buffered(1)),
                  pl.BlockSpec((hp, SEQ // t, vrows, t), lambda h, i: (h, 0, 0, 0),
                               pipeline_mode=pl.Buffered(1))],
        out_specs=pl.BlockSpec((t, hp * MLA_V), lambda h, i: (i, h)),
        out_shape=jax.ShapeDtypeStruct((SEQ, MLA_HEADS * MLA_V), BF16),
        scratch_shapes=[pltpu.VMEM((hp, SUBLANES, t), F32), pltpu.VMEM((hp, vrows, t), F32)],
        compiler_params=_params(("parallel", "arbitrary")),
        name="flash",
    )(qt, k, vt)


def _sconv_kernel(x_ref, gmix_ref, w_ref, wsc_ref, o_ref, buf):
    tm = ROW_TILE
    h = _normed_bf16(x_ref, gmix_ref)
    z = _rdot(h, w_ref[...])

    @pl.when(pl.program_id(0) == 0)
    def _():
        buf[0:SUBLANES, :] = jnp.zeros((SUBLANES, GROUP), F32)

    g = z[:, GROUP:2 * GROUP] * z[:, 2 * GROUP:]
    buf[SUBLANES:, :] = g
    w = wsc_ref[...]
    conv = (w[2:3, :] * g + w[1:2, :] * buf[SUBLANES - 1:SUBLANES - 1 + tm, :]
            + w[0:1, :] * buf[SUBLANES - 2:SUBLANES - 2 + tm, :])
    o_ref[...] = (z[:, :GROUP] * conv).astype(o_ref.dtype)
    buf[0:SUBLANES, :] = buf[tm:tm + SUBLANES, :]


def _sconv(l, x, g_mix, w_in, wsc):
    tm = ROW_TILE
    return pl.pallas_call(
        _sconv_kernel,
        grid=(SEQ // tm,),
        in_specs=[pl.BlockSpec((tm, D_MODEL), lambda i: (i, 0)),
                  _lay(g_mix, l), _w_in_spec(l, "sconv"), _lay(wsc, l)],
        out_specs=pl.BlockSpec((tm, GROUP), lambda i: (i, 0)),
        out_shape=jax.ShapeDtypeStruct((SEQ, GROUP), BF16),
        scratch_shapes=[pltpu.VMEM((tm + SUBLANES, GROUP), F32)],
        compiler_params=_params(("arbitrary",)),
        name="sconv",
    )(x, g_mix, w_in, wsc)


def _conformer_kernel(x_ref, gmix_ref, w_ref, wcf_ref, bcf_ref, gln_ref, bln_ref, o_ref, buf, wbuf, uscr):
    tm = ROW_TILE
    nslab = GROUP // LANES
    h = _normed_bf16(x_ref, gmix_ref)
    z = _rdot(h, w_ref[...])

    @pl.when(pl.program_id(0) == 0)
    def _():
        buf[:, 0:CF_HALO, :] = jnp.zeros((nslab, CF_HALO, LANES), F32)

    glu = z[:, :GROUP] * jax.nn.sigmoid(z[:, GROUP:])
    for sl in range(nslab):
        buf[sl, CF_HALO:, :] = glu[:, sl * LANES:(sl + 1) * LANES]
    base = CF_HALO - (CF_K - 1)
    for phase in range(1, SUBLANES):
        wbuf[phase - 1] = buf[:, phase:phase + tm + CF_HALO - SUBLANES, :]

    def tap_rows(o, r0, sl):
        phase = o % SUBLANES
        start = pl.multiple_of(r0 + (o - phase), SUBLANES)
        if phase == 0:
            return buf[sl, pl.ds(start, CF_ROWS), :]
        return wbuf[phase - 1, sl, pl.ds(start, CF_ROWS), :]

    def row_block(rb, carry):
        r0 = pl.multiple_of(rb * CF_ROWS, CF_ROWS)
        for sl in range(nslab):
            lanes = slice(sl * LANES, (sl + 1) * LANES)
            acc = jnp.broadcast_to(bcf_ref[:, lanes], (CF_ROWS, LANES))
            for j in range(CF_K):
                acc = acc + wcf_ref[j:j + 1, lanes] * tap_rows(base + j, r0, sl)
            uscr[pl.ds(r0, CF_ROWS), lanes] = acc
        return carry

    lax.fori_loop(0, tm // CF_ROWS, row_block, 0)
    u = uscr[...]
    mu = jnp.mean(u, axis=-1, keepdims=True)
    uc = u - mu
    y = uc * lax.rsqrt(jnp.mean(uc * uc, axis=-1, keepdims=True) + EPS)
    y = y * gln_ref[...] + bln_ref[...]
    o_ref[...] = (y * jax.nn.sigmoid(y)).astype(o_ref.dtype)
    buf[:, 0:CF_HALO, :] = buf[:, tm:tm + CF_HALO, :]


def _conformer(l, x, g_mix, w_in, wcf, bcf, gln, bln):
    tm = ROW_TILE
    return pl.pallas_call(
        _conformer_kernel,
        grid=(SEQ // tm,),
        in_specs=[pl.BlockSpec((tm, D_MODEL), lambda i: (i, 0)), _lay(g_mix, l),
                  _w_in_spec(l, "conformer")] + [_lay(a, l) for a in (wcf, bcf, gln, bln)],
        out_specs=pl.BlockSpec((tm, GROUP), lambda i: (i, 0)),
        out_shape=jax.ShapeDtypeStruct((SEQ, GROUP), BF16),
        scratch_shapes=[pltpu.VMEM((GROUP // LANES, tm + CF_HALO, LANES), F32),
                        pltpu.VMEM((SUBLANES - 1, GROUP // LANES, tm + CF_HALO - SUBLANES, LANES), F32),
                        pltpu.VMEM((tm, GROUP), F32)],
        compiler_params=_params(("arbitrary",)),
        name="conformer",
    )(x, g_mix, w_in, wcf, bcf, gln, bln)


def _retention_kernel(x_ref, pos_ref, gmix_ref, w_ref, freq_ref, qmask_ref, kdec_ref, qdec_ref,
                      inner_ref, cdec_ref, gret_ref, o_ref, state):
    tm = ROW_TILE
    c = RET_CHUNK
    dk2 = RET_HEADS * RET_DK
    h = _normed_bf16(x_ref, gmix_ref)
    z = _rdot(h, w_ref[...])

    @pl.when(pl.program_id(0) == 0)
    def _():
        state[...] = jnp.zeros(state.shape, F32)

    ang = pos_ref[...] * freq_ref[...]
    cos_t = jnp.cos(ang)
    sin_t = jnp.sin(ang)

    def rot(t):
        t1, t2 = t[:, :LANES], t[:, LANES:]
        return jnp.concatenate([t1 * cos_t - t2 * sin_t, t2 * cos_t + t1 * sin_t], axis=1)

    q = rot(z[:, :dk2])
    k = rot(z[:, dk2:2 * dk2]) * (RET_DK ** -0.5)
    v = z[:, 2 * dk2:2 * dk2 + GROUP]
    gate = z[:, 2 * dk2 + GROUP:]

    for ci in range(tm // c):
        rows = slice(ci * c, (ci + 1) * c)
        qc = q[rows]
        kt = k[rows].T
        kt_b = kt.astype(BF16)
        for hd in range(RET_HEADS):
            cols = slice(hd * RET_DV, (hd + 1) * RET_DV)
            qm = (qc * qmask_ref[hd]).astype(BF16)
            vh = v[rows, cols].astype(BF16)
            sc = _dot(qm, kt_b) * inner_ref[hd]
            st = state[hd]
            o = _dot(sc.astype(BF16), vh) + _dot(qm, st.astype(BF16)) * qdec_ref[hd]
            state[hd] = st * cdec_ref[hd] + _dot((kt * kdec_ref[hd]).astype(BF16), vh)
            on = o * _rms_scale(o, RET_DV) * gret_ref[hd]
            gh = gate[rows, cols]
            o_ref[rows, cols] = (gh * jax.nn.sigmoid(gh) * on).astype(o_ref.dtype)


def _retention(l, x, posf, g_mix, w_in, freq, qmask, kdec, qdec, inner, cdec, gret):
    tm = ROW_TILE
    return pl.pallas_call(
        _retention_kernel,
        grid=(SEQ // tm,),
        in_specs=[pl.BlockSpec((tm, D_MODEL), lambda i: (i, 0)),
                  pl.BlockSpec((tm, 1), lambda i: (i, 0)), _lay(g_mix, l), _w_in_spec(l, "retention")]
        + [_const(a) for a in (freq, qmask, kdec, qdec, inner, cdec)] + [_lay(gret, l)],
        out_specs=pl.BlockSpec((tm, GROUP), lambda i: (i, 0)),
        out_shape=jax.ShapeDtypeStruct((SEQ, GROUP), BF16),
        scratch_shapes=[pltpu.VMEM((RET_HEADS, RET_HEADS * RET_DK, RET_DV), F32)],
        compiler_params=_params(("arbitrary",)),
        name="retention",
    )(x, posf, g_mix, w_in, freq, qmask, kdec, qdec, inner, cdec, gret)


def _out_proj_kernel(x_ref, ya_ref, yb_ref, yc_ref, yd_ref, wo_ref, o_ref):
    acc = x_ref[...]
    for n, y_ref in enumerate((ya_ref, yb_ref, yc_ref, yd_ref)):
        acc = acc + _dot(y_ref[...], wo_ref[n * GROUP:(n + 1) * GROUP, :])
    o_ref[...] = acc


def _out_proj(l, x, ya, yb, yc, yd, wo):
    tm = ROW_TILE
    row = lambda n: pl.BlockSpec((tm, n), lambda i: (i, 0))
    return pl.pallas_call(
        _out_proj_kernel,
        grid=(SEQ // tm,),
        in_specs=[row(D_MODEL), row(GROUP), row(GROUP), row(GROUP), row(GROUP), _lay(wo, l)],
        out_specs=row(D_MODEL),
        out_shape=jax.ShapeDtypeStruct((SEQ, D_MODEL), F32),
        compiler_params=_params(("parallel",)),
        name="out_proj",
    )(x, ya, yb, yc, yd, wo)


def _ffn_kernel(x_ref, g_ref, wg_ref, wu_ref, cg_ref, cu_ref, wd_ref, o_ref, hn, ubuf, carry):
    tm = ROW_TILE
    i = pl.program_id(0)
    j = pl.program_id(1)

    @pl.when(j == 0)
    def _():
        hn[...] = _normed_bf16(x_ref, g_ref)
        o_ref[...] = x_ref[...]

    @pl.when(i == 0)
    def _():
        carry[j] = jnp.zeros(carry.shape[1:], F32)

    ts = FF_TILE // FF_SUB
    h = hn[...]

    def conv(part, s, u, c_ref):
        cols = slice(s * ts, (s + 1) * ts)
        ubuf[part, s, 0:SUBLANES, :] = carry[j, part, :, cols]
        ubuf[part, s, SUBLANES:, :] = u
        carry[j, part, :, cols] = u[tm - SUBLANES:, :]
        w = c_ref[:, cols]
        return (w[2:3, :] * u + w[1:2, :] * ubuf[part, s, SUBLANES - 1:SUBLANES - 1 + tm, :]
                + w[0:1, :] * ubuf[part, s, SUBLANES - 2:SUBLANES - 2 + tm, :])

    us = [(_rdot(h, wg_ref[:, s * ts:(s + 1) * ts]), _rdot(h, wu_ref[:, s * ts:(s + 1) * ts]))
          for s in range(FF_SUB)]
    for s, (ug, uu) in enumerate(us):
        gate = conv(0, s, ug, cg_ref)
        up = conv(1, s, uu, cu_ref)
        act = (gate * jax.nn.sigmoid(gate) * up).astype(BF16)
        o_ref[...] += _dot(act, wd_ref[s * ts:(s + 1) * ts, :])


def _ffn(l, x, g, w_up, w_conv, w_down):
    tm, tf = ROW_TILE, FF_TILE
    nf = D_FF // tf
    return pl.pallas_call(
        _ffn_kernel,
        grid=(SEQ // tm, nf),
        in_specs=[pl.BlockSpec((tm, D_MODEL), lambda i, j: (i, 0)),
                  _lay(g, l),
                  pl.BlockSpec((None, D_MODEL, tf), lambda i, j: (l, 0, j)),
                  pl.BlockSpec((None, D_MODEL, tf), lambda i, j: (l, 0, j + nf)),
                  pl.BlockSpec((None, FFN_K, tf), lambda i, j: (l, 0, j)),
                  pl.BlockSpec((None, FFN_K, tf), lambda i, j: (l, 0, j + nf)),
                  pl.BlockSpec((None, tf, D_MODEL), lambda i, j: (l, j, 0))],
        out_specs=pl.BlockSpec((tm, D_MODEL), lambda i, j: (i, 0)),
        out_shape=jax.ShapeDtypeStruct((SEQ, D_MODEL), F32),
        scratch_shapes=[pltpu.VMEM((tm, D_MODEL), BF16),
                        pltpu.VMEM((2, FF_SUB, tm + SUBLANES, tf // FF_SUB), F32),
                        pltpu.VMEM((nf, 2, SUBLANES, tf), F32)],
        compiler_params=_params(("arbitrary", "arbitrary")),
        name="ffn",
    )(x, g, w_up, w_up, w_conv, w_conv, w_down)


def _ple_kernel(x_ref, p_ref, g_ref, wpg_ref, wpe_ref, o_ref):
    hn = _normed_bf16(x_ref, g_ref)
    pb = p_ref[...].astype(BF16)
    for n in range(D_MODEL // GROUP):
        cols = slice(n * GROUP, (n + 1) * GROUP)
        gate = jax.nn.sigmoid(_dot(hn, wpg_ref[:, cols]))
        o_ref[:, cols] = x_ref[:, cols] + _dot(pb, wpe_ref[:, cols]) * gate


def _ple(l, x, p, g, wpg, wpe):
    tm = ROW_TILE
    row = lambda n: pl.BlockSpec((tm, n), lambda i: (i, 0))
    return pl.pallas_call(
        _ple_kernel,
        grid=(SEQ // tm,),
        in_specs=[row(D_MODEL), pl.BlockSpec((None, None, tm, PLE_DIM), lambda i: (l, 0, i, 0)),
                  _lay(g, l), _lay(wpg, l), _lay(wpe, l)],
        out_specs=row(D_MODEL),
        out_shape=jax.ShapeDtypeStruct((SEQ, D_MODEL), F32),
        compiler_params=_params(("parallel",)),
        name="ple",
    )(x, p, g, wpg, wpe)


def _retention_tables():
    c = RET_CHUNK
    lg = jnp.log(1.0 - 2.0 ** (-5.0 - jnp.arange(RET_HEADS, dtype=F32)))
    idx = jnp.arange(c, dtype=F32)
    rel = idx[:, None] - idx[None, :]
    inner = jnp.where(rel[None] >= 0, jnp.exp(lg[:, None, None] * rel[None]), 0.0)
    qdec = jnp.exp(lg[:, None] * (idx + 1.0))[:, :, None]
    kdec_t = jnp.exp(lg[:, None] * (c - 1.0 - idx))
    cdec = jnp.exp(lg * c)
    lane = np.arange(RET_HEADS * RET_DK)
    head_of_lane = (lane % LANES) // (RET_DK // 2)
    hmask = jnp.asarray(head_of_lane[None, :] == np.arange(RET_HEADS)[:, None], F32)
    qmask = hmask[:, None, :]
    kdec = hmask[:, :, None] * kdec_t[:, None, :]
    cdec_b = jnp.broadcast_to(cdec[:, None, None], (RET_HEADS, 1, RET_DV))
    return qmask, kdec, qdec, inner, cdec_b


def _lane_freq(inv):
    return jnp.tile(inv, LANES // inv.shape[0])[None, :]


def kernel(x, p, positions, g_mix, w_in, g_qa, g_kva, w_q_up, w_kv_up, g_qn, g_kn, w_sc, w_cf,
           b_cf, g_cf_ln, b_cf_ln, g_ret, w_o, g_ffn, w_up, w_ffn_conv, w_down, g_pe, w_pe, w_pg):
    assert x.shape == (1, SEQ, D_MODEL)
    xs = x[0]
    posf = positions[0].astype(F32)[:, None]
    rows = lambda a: a[:, None, :]

    mla_freq = _lane_freq(ROPE_BASE ** (-jnp.arange(0, MLA_ROPE, 2, dtype=F32) / MLA_ROPE))
    ret_freq = _lane_freq(1.0 / (10000.0 ** jnp.linspace(0.0, 1.0, RET_DK // 2, dtype=F32)))
    qmask, kdec, qdec, inner, cdec = _retention_tables()

    o_mla, o_sc, o_cf, o_ret = np.cumsum([0, Q_LORA + KV_LORA + MLA_ROPE, 3 * GROUP, 2 * GROUP])
    dk2 = RET_HEADS * RET_DK
    half = RET_DK // 2

    def rotary_halves_first(w):
        return w.reshape(DEPTH, D_MODEL, RET_HEADS, 2, half).transpose(0, 1, 3, 2, 4).reshape(DEPTH, D_MODEL, dk2)

    w_ret = w_in[:, :, o_ret:]
    w_in_r = jnp.concatenate(
        [w_in[:, :, o_sc:o_cf],
         rotary_halves_first(w_ret[:, :, :dk2]), rotary_halves_first(w_ret[:, :, dk2:2 * dk2]),
         w_ret[:, :, 2 * dk2:],
         w_in[:, :, o_cf:o_ret],
         w_in[:, :, :o_sc], jnp.zeros((DEPTH, D_MODEL, W_IN_COLS - N_IN), F32)], axis=2).astype(BF16)
    wq = jnp.pad(w_q_up.reshape(DEPTH, Q_LORA, MLA_HEADS, MLA_QK),
                 ((0, 0), (0, 0), (0, 0), (0, HEAD_PAD - MLA_QK))
                 ).reshape(DEPTH, Q_LORA, MLA_HEADS * HEAD_PAD).astype(BF16)
    wkv4 = w_kv_up.reshape(DEPTH, KV_LORA, MLA_HEADS, MLA_NOPE + MLA_V)
    wkv = jnp.concatenate([wkv4[..., :MLA_NOPE].reshape(DEPTH, KV_LORA, -1),
                           wkv4[..., MLA_NOPE:].reshape(DEPTH, KV_LORA, -1)], axis=2).astype(BF16)
    gq = jnp.pad(g_qn * (MLA_QK ** -0.5 * LOG2_E), ((0, 0), (0, HEAD_PAD - MLA_QK)))
    gkr = jnp.pad(g_kn[:, MLA_NOPE:], ((0, 0), (0, LANES - MLA_ROPE)))
    w_o_b, w_up_b, w_down_b = w_o.astype(BF16), w_up.astype(BF16), w_down.astype(BF16)
    w_pg_b, w_pe_b = w_pg.astype(BF16), w_pe.astype(BF16)
    gm, g_ret4 = rows(g_mix), g_ret[:, :, None, :]

    for l in range(DEPTH):
        qt, k, vt = _mla_pre(l, xs, posf, gm, w_in_r, rows(g_qa), rows(g_kva), wq, wkv, rows(gq),
                             rows(g_kn[:, :MLA_NOPE]), rows(gkr), mla_freq)
        y_a = _flash(qt, k, vt)
        y_b = _sconv(l, xs, gm, w_in_r, w_sc)
        y_c = _conformer(l, xs, gm, w_in_r, w_cf, rows(b_cf), rows(g_cf_ln), rows(b_cf_ln))
        y_d = _retention(l, xs, posf, gm, w_in_r, ret_freq, qmask, kdec, qdec, inner, cdec, g_ret4)
        xs = _out_proj(l, xs, y_a, y_b, y_c, y_d, w_o_b)
        xs = _ffn(l, xs, rows(g_ffn), w_up_b, w_ffn_conv, w_down_b)
        xs = _ple(l, xs, p, rows(g_pe), w_pg_b, w_pe_b)
    return xs[None]
```

```python
import numpy as np
import jax
import jax.numpy as jnp
from jax import lax
from jax.experimental import pallas as pl
from jax.experimental.pallas import tpu as pltpu

F32 = jnp.float32
BF16 = jnp.bfloat16

D_MODEL = 2048
SEQ = 8192
DEPTH = 2
GROUP = 512
MLA_HEADS = 4
MLA_NOPE = 128
MLA_ROPE = 64
MLA_V = 128
MLA_QK = MLA_NOPE + MLA_ROPE
Q_LORA = 512
KV_LORA = 256
ROPE_BASE = 10000.0
SC_K = 3
CF_K = 31
RET_HEADS = 4
RET_DV = 128
RET_DK = 64
RET_CHUNK = 128
D_FF = 5632
FFN_K = 3
PLE_DIM = 256
EPS = 1e-6
N_IN = Q_LORA + KV_LORA + MLA_ROPE + 3 * GROUP + 2 * GROUP + 2 * RET_HEADS * (RET_DK + RET_DV)

LANES = 128
SUBLANES = 8
HEAD_PAD = 2 * LANES
ROW_TILE = 512
FF_TILE = 512
FF_SUB = 2
DOT_ROWS = 128
ATT_TILE = 512
ATT_HEADS_PER_STEP = 4
V_ONES = 16
CF_HALO = 32
CF_ROWS = 128
MLA_IN = Q_LORA + KV_LORA + LANES
W_IN_SHIFT = 3 * GROUP - (Q_LORA + KV_LORA + MLA_ROPE)
W_IN_GROUPS = {"sconv": (3 * GROUP, 1), "conformer": (2 * GROUP, 3), "ret_qk": (GROUP, 8),
               "ret_v": (GROUP, 9), "ret_gate": (GROUP, 10)}
VMEM_LIMIT = 56 * 1024 * 1024
MASK_VALUE = -1e30
LOG2_E = 1.4426950408889634


def _lay(a, l):
    return pl.BlockSpec((None,) + a.shape[1:], lambda *_: (l,) + (0,) * (a.ndim - 1))


def _const(a):
    return pl.BlockSpec(a.shape, lambda *_: (0,) * a.ndim)


def _w_in_spec(l, group):
    width, block = W_IN_GROUPS[group]
    return pl.BlockSpec((None, D_MODEL, width), lambda *_: (l, 0, block))


def _params(sem):
    return pltpu.CompilerParams(dimension_semantics=sem, vmem_limit_bytes=VMEM_LIMIT)


def _rms_scale(xf, n):
    return lax.rsqrt(jnp.sum(xf * xf, axis=-1, keepdims=True) * (1.0 / n) + EPS)


def _normed_bf16(xf, g_ref):
    return (xf * _rms_scale(xf, D_MODEL) * g_ref[...]).astype(BF16)


def _prenorm_kernel(x_ref, g_ref, h_out):
    h_out[...] = _normed_bf16(x_ref[...], g_ref)


def _prenorm(l, x, g):
    tm = ROW_TILE
    row = pl.BlockSpec((tm, D_MODEL), lambda i: (i, 0))
    return pl.pallas_call(
        _prenorm_kernel,
        grid=(SEQ // tm,),
        in_specs=[row, _lay(g, l)],
        out_specs=row,
        out_shape=jax.ShapeDtypeStruct((SEQ, D_MODEL), BF16),
        compiler_params=_params(("parallel",)),
        name="prenorm",
    )(x, g)


def _dot(a, b):
    return jnp.dot(a, b, preferred_element_type=F32)


def _rdot(a, b):
    n = a.shape[0] // DOT_ROWS
    return jnp.concatenate([_dot(a[r * DOT_ROWS:(r + 1) * DOT_ROWS], b) for r in range(n)], axis=0)


def _rope_slab(x, cos_t, sin_a, sin_b):
    return (x * cos_t + pltpu.roll(x, LANES - 32, axis=1) * sin_a
            + pltpu.roll(x, 32, axis=1) * sin_b)


def _mla_pre_kernel(h_ref, pos_ref, wa_ref, gqa_ref, gkva_ref, wq_ref, wkv_ref,
                    gq_ref, gkn_ref, gkr_ref, freq_ref, qt_out, k_out, vt_out):
    z = _rdot(h_ref[...], wa_ref[...])
    zq = z[:, :Q_LORA]
    zkv = z[:, Q_LORA:Q_LORA + KV_LORA]
    lane = lax.broadcasted_iota(jnp.int32, (z.shape[0], LANES), 1)
    zkr = jnp.where(lane < MLA_ROPE, z[:, Q_LORA + KV_LORA:], 0.0)
    c_q = (zq * _rms_scale(zq, Q_LORA) * gqa_ref[...]).astype(BF16)
    c_kv = (zkv * _rms_scale(zkv, KV_LORA) * gkva_ref[...]).astype(BF16)

    ang = pos_ref[...] * freq_ref[...]
    cos_t = jnp.cos(ang)
    sin_t = jnp.sin(ang)
    sin_a = jnp.where(lane < 32, -sin_t, 0.0)
    sin_b = jnp.where((lane >= 32) & (lane < 64), sin_t, 0.0)

    q = _rdot(c_q, wq_ref[...])
    kv = _rdot(c_kv, wkv_ref[...])

    kr = zkr * gkr_ref[...]
    ss_kr = jnp.sum(zkr * zkr, axis=-1, keepdims=True)
    kr_rot = _rope_slab(kr, cos_t, sin_a, sin_b)
    ones = jnp.ones((V_ONES, z.shape[0]), BF16)
    for hd in range(MLA_HEADS):
        qb = q[:, hd * HEAD_PAD:(hd + 1) * HEAD_PAD]
        qn = qb * _rms_scale(qb, MLA_QK) * gq_ref[...]
        qt_out[hd, 0:LANES, :] = qn[:, :LANES].T.astype(BF16)
        qt_out[hd, LANES:, :] = _rope_slab(qn[:, LANES:], cos_t, sin_a, sin_b).T.astype(BF16)
        vh = kv[:, MLA_HEADS * MLA_NOPE + hd * MLA_V:MLA_HEADS * MLA_NOPE + (hd + 1) * MLA_V]
        vt_out[hd, 0, 0:MLA_V, :] = vh.T.astype(BF16)
        vt_out[hd, 0, MLA_V:, :] = ones
        kn = kv[:, hd * MLA_NOPE:(hd + 1) * MLA_NOPE]
        r_k = lax.rsqrt((jnp.sum(kn * kn, axis=-1, keepdims=True) + ss_kr) * (1.0 / MLA_QK) + EPS)
        k_out[:, hd * HEAD_PAD:hd * HEAD_PAD + LANES] = (kn * r_k * gkn_ref[...]).astype(BF16)
        k_out[:, hd * HEAD_PAD + LANES:(hd + 1) * HEAD_PAD] = (kr_rot * r_k).astype(BF16)


def _mla_pre(l, h, posf, wa, g_qa, g_kva, wq, wkv, gq, gkn, gkr, freq):
    tm = ROW_TILE
    row = lambda n: pl.BlockSpec((tm, n), lambda i: (i, 0))
    return pl.pallas_call(
        _mla_pre_kernel,
        grid=(SEQ // tm,),
        in_specs=[row(D_MODEL), row(1)]
        + [_lay(a, l) for a in (wa, g_qa, g_kva, wq, wkv, gq, gkn, gkr)] + [_const(freq)],
        out_specs=[pl.BlockSpec((MLA_HEADS, HEAD_PAD, tm), lambda i: (0, 0, i)),
                   row(MLA_HEADS * HEAD_PAD),
                   pl.BlockSpec((MLA_HEADS, 1, MLA_V + V_ONES, tm), lambda i: (0, i, 0, 0))],
        out_shape=[jax.ShapeDtypeStruct((MLA_HEADS, HEAD_PAD, SEQ), BF16),
                   jax.ShapeDtypeStruct((SEQ, MLA_HEADS * HEAD_PAD), BF16),
                   jax.ShapeDtypeStruct((MLA_HEADS, SEQ // tm, MLA_V + V_ONES, tm), BF16)],
        compiler_params=_params(("parallel",)),
        name="mla_pre",
    )(h, posf, wa, g_qa, g_kva, wq, wkv, gq, gkn, gkr, freq)


def _flash_kernel(qt_ref, k_ref, vt_ref, o_ref, m_scr, acc_scr):
    t = ATT_TILE
    i = pl.program_id(1)
    sub = SUBLANES
    pack = 2 * SUBLANES
    vrows = MLA_V + V_ONES
    m_scr[...] = jnp.full(m_scr.shape, MASK_VALUE, F32)
    acc_scr[...] = jnp.zeros(acc_scr.shape, F32)

    def block(j, masked):
        start = pl.multiple_of(j * t, t)
        if masked:
            key = lax.broadcasted_iota(jnp.int32, (t, t), 0)
            qry = lax.broadcasted_iota(jnp.int32, (t, t), 1)
            keep = key <= qry
        sts = [_rdot(k_ref[pl.ds(start, t), hh * HEAD_PAD:(hh + 1) * HEAD_PAD], qt_ref[hh])
               for hh in range(ATT_HEADS_PER_STEP)]
        for hh, st in enumerate(sts):
            if masked:
                st = jnp.where(keep, st, MASK_VALUE)
            s3 = st.reshape(t // pack, pack, t)
            mx = jnp.max(s3, axis=0)
            mx = jnp.maximum(mx[:sub], mx[sub:])
            for shift in (4, 2, 1):
                mx = jnp.maximum(mx, pltpu.roll(mx, shift, axis=0))
            m_prev = m_scr[hh]
            m_new = jnp.maximum(m_prev, mx)
            alpha = jnp.exp2(m_prev - m_new)
            m2 = jnp.concatenate([m_new, m_new], axis=0)
            p = jnp.exp2((s3 - m2[None]).astype(BF16)).reshape(t, t)
            pv = _dot(vt_ref[hh, j], p)
            acc = acc_scr[hh].reshape(vrows // sub, sub, t) * alpha[None] + pv.reshape(vrows // sub, sub, t)
            acc_scr[hh] = acc.reshape(vrows, t)
            m_scr[hh] = m_new

    def body(j, carry):
        block(j, False)
        return carry

    lax.fori_loop(0, i, body, 0)
    block(i, True)
    for hh in range(ATT_HEADS_PER_STEP):
        num = acc_scr[hh, 0:MLA_V, :].reshape(MLA_V // sub, sub, t)
        den = acc_scr[hh, MLA_V:MLA_V + sub, :]
        out_t = (num / den[None]).reshape(MLA_V, t)
        o_ref[:, hh * MLA_V:(hh + 1) * MLA_V] = out_t.T.astype(o_ref.dtype)


def _flash(qt, k, vt):
    t, hp = ATT_TILE, ATT_HEADS_PER_STEP
    vrows = MLA_V + V_ONES
    return pl.pallas_call(
        _flash_kernel,
        grid=(MLA_HEADS // hp, SEQ // t),
        in_specs=[pl.BlockSpec((hp, HEAD_PAD, t), lambda h, i: (h, 0, i)),
                  pl.BlockSpec((SEQ, hp * HEAD_PAD), lambda h, i: (0, h), pipeline_mode=pl.Buffered(1)),
                  pl.BlockSpec((hp, SEQ // t, vrows, t), lambda h, i: (h, 0, 0, 0),
                               pipeline_mode=pl.Buffered(1))],
        out_specs=pl.BlockSpec((t, hp * MLA_V), lambda h, i: (i, h)),
        out_shape=jax.ShapeDtypeStruct((SEQ, MLA_HEADS * MLA_V), BF16),
        scratch_shapes=[pltpu.VMEM((hp, SUBLANES, t), F32), pltpu.VMEM((hp, vrows, t), F32)],
        compiler_params=_params(("parallel", "arbitrary")),
        name="flash",
    )(qt, k, vt)


def _sconv_kernel(h_ref, w_ref, wsc_ref, o_ref, buf):
    tm = ROW_TILE
    z = _rdot(h_ref[...], w_ref[...])

    @pl.when(pl.program_id(0) == 0)
    def _():
        buf[0:SUBLANES, :] = jnp.zeros((SUBLANES, GROUP), F32)

    g = z[:, GROUP:2 * GROUP] * z[:, 2 * GROUP:]
    buf[SUBLANES:, :] = g
    w = wsc_ref[...]
    conv = (w[2:3, :] * g + w[1:2, :] * buf[SUBLANES - 1:SUBLANES - 1 + tm, :]
            + w[0:1, :] * buf[SUBLANES - 2:SUBLANES - 2 + tm, :])
    o_ref[...] = (z[:, :GROUP] * conv).astype(o_ref.dtype)
    buf[0:SUBLANES, :] = buf[tm:tm + SUBLANES, :]


def _sconv(l, h, w_in, wsc):
    tm = ROW_TILE
    return pl.pallas_call(
        _sconv_kernel,
        grid=(SEQ // tm,),
        in_specs=[pl.BlockSpec((tm, D_MODEL), lambda i: (i, 0)),
                  _w_in_spec(l, "sconv"), _lay(wsc, l)],
        out_specs=pl.BlockSpec((tm, GROUP), lambda i: (i, 0)),
        out_shape=jax.ShapeDtypeStruct((SEQ, GROUP), BF16),
        scratch_shapes=[pltpu.VMEM((tm + SUBLANES, GROUP), F32)],
        compiler_params=_params(("arbitrary",)),
        name="sconv",
    )(h, w_in, wsc)


def _conformer_kernel(h_ref, w_ref, wcf_ref, bcf_ref, gln_ref, bln_ref, o_ref, buf, wbuf, uscr):
    tm = ROW_TILE
    nslab = GROUP // LANES
    z = _rdot(h_ref[...], w_ref[...])

    @pl.when(pl.program_id(0) == 0)
    def _():
        buf[:, 0:CF_HALO, :] = jnp.zeros((nslab, CF_HALO, LANES), F32)

    glu = z[:, :GROUP] * jax.nn.sigmoid(z[:, GROUP:])
    for sl in range(nslab):
        buf[sl, CF_HALO:, :] = glu[:, sl * LANES:(sl + 1) * LANES]
    base = CF_HALO - (CF_K - 1)
    for phase in range(1, SUBLANES):
        wbuf[phase - 1] = buf[:, phase:phase + tm + CF_HALO - SUBLANES, :]

    def tap_rows(o, r0, sl):
        phase = o % SUBLANES
        start = pl.multiple_of(r0 + (o - phase), SUBLANES)
        if phase == 0:
            return buf[sl, pl.ds(start, CF_ROWS), :]
        return wbuf[phase - 1, sl, pl.ds(start, CF_ROWS), :]

    def row_block(rb, carry):
        r0 = pl.multiple_of(rb * CF_ROWS, CF_ROWS)
        for sl in range(nslab):
            lanes = slice(sl * LANES, (sl + 1) * LANES)
            acc = jnp.broadcast_to(bcf_ref[:, lanes], (CF_ROWS, LANES))
            for j in range(CF_K):
                acc = acc + wcf_ref[j:j + 1, lanes] * tap_rows(base + j, r0, sl)
            uscr[pl.ds(r0, CF_ROWS), lanes] = acc
        return carry

    lax.fori_loop(0, tm // CF_ROWS, row_block, 0)
    u = uscr[...]
    mu = jnp.mean(u, axis=-1, keepdims=True)
    uc = u - mu
    y = uc * lax.rsqrt(jnp.mean(uc * uc, axis=-1, keepdims=True) + EPS)
    y = y * gln_ref[...] + bln_ref[...]
    o_ref[...] = (y * jax.nn.sigmoid(y)).astype(o_ref.dtype)
    buf[:, 0:CF_HALO, :] = buf[:, tm:tm + CF_HALO, :]


def _conformer(l, h, w_in, wcf, bcf, gln, bln):
    tm = ROW_TILE
    return pl.pallas_call(
        _conformer_kernel,
        grid=(SEQ // tm,),
        in_specs=[pl.BlockSpec((tm, D_MODEL), lambda i: (i, 0)), _w_in_spec(l, "conformer")] + [_lay(a, l) for a in (wcf, bcf, gln, bln)],
        out_specs=pl.BlockSpec((tm, GROUP), lambda i: (i, 0)),
        out_shape=jax.ShapeDtypeStruct((SEQ, GROUP), BF16),
        scratch_shapes=[pltpu.VMEM((GROUP // LANES, tm + CF_HALO, LANES), F32),
                        pltpu.VMEM((SUBLANES - 1, GROUP // LANES, tm + CF_HALO - SUBLANES, LANES), F32),
                        pltpu.VMEM((tm, GROUP), F32)],
        compiler_params=_params(("arbitrary",)),
        name="conformer",
    )(h, w_in, wcf, bcf, gln, bln)


def _convmix_kernel(h_ref, wsc_ref, wcf_ref, tsc_ref, tcf_ref, bcf_ref, gln_ref, bln_ref,
                    yb_out, yc_out, stail, buf, wbuf, uscr):
    tm = ROW_TILE
    nslab = GROUP // LANES
    base = CF_HALO - (CF_K - 1)
    head = CF_HALO - SUBLANES

    @pl.when(pl.program_id(0) == 0)
    def _():
        stail[...] = jnp.zeros(stail.shape, F32)
        buf[:, 0:CF_HALO, :] = jnp.zeros((nslab, CF_HALO, LANES), F32)

    for phase in range(1, SUBLANES):
        wbuf[phase - 1, :, 0:head, :] = buf[:, phase:phase + head, :]

    row8 = lax.broadcasted_iota(jnp.int32, (SUBLANES, GROUP), 0)

    def shifted(g, tail, n):
        rolled = pltpu.roll(g, n, axis=0)
        top = jnp.where(row8 < n, pltpu.roll(tail, n, axis=0), rolled[:SUBLANES])
        return jnp.concatenate([top, rolled[SUBLANES:]], axis=0)

    def row_block(rb, carry):
        r0 = pl.multiple_of(rb * CF_ROWS, CF_ROWS)
        hb = h_ref[pl.ds(r0, CF_ROWS), :]
        zc = _dot(hb, wcf_ref[...])
        glu = zc[:, :GROUP] * jax.nn.sigmoid(zc[:, GROUP:])
        for sl in range(nslab):
            buf[sl, pl.ds(r0 + CF_HALO, CF_ROWS), :] = glu[:, sl * LANES:(sl + 1) * LANES]
        for phase in range(1, SUBLANES):
            for sl in range(nslab):
                wbuf[phase - 1, sl, pl.ds(r0 + head, CF_ROWS), :] = (
                    buf[sl, pl.ds(r0 + head, CF_ROWS + SUBLANES), :][phase:phase + CF_ROWS])
        zs_parts = []
        nsplit = 3 * GROUP // (2 * LANES)
        for sl in range(nslab):
            for n in range(sl * nsplit // nslab, (sl + 1) * nsplit // nslab):
                zs_parts.append(_dot(hb, wsc_ref[:, n * 2 * LANES:(n + 1) * 2 * LANES]))
            lanes = slice(sl * LANES, (sl + 1) * LANES)
            acc = jnp.broadcast_to(bcf_ref[:, lanes], (CF_ROWS, LANES))
            for j in range(CF_K):
                o = base + j
                phase = o % SUBLANES
                start = pl.multiple_of(r0 + (o - phase), SUBLANES)
                tap = (buf[sl, pl.ds(start, CF_ROWS), :] if phase == 0
                       else wbuf[phase - 1, sl, pl.ds(start, CF_ROWS), :])
                acc = acc + tcf_ref[j:j + 1, lanes] * tap
            uscr[:, lanes] = acc
        u = uscr[...]
        mu = jnp.mean(u, axis=-1, keepdims=True)
        uc = u - mu
        y = uc * lax.rsqrt(jnp.mean(uc * uc, axis=-1, keepdims=True) + EPS)
        y = y * gln_ref[...] + bln_ref[...]
        yc_out[pl.ds(r0, CF_ROWS), :] = (y * jax.nn.sigmoid(y)).astype(yc_out.dtype)
        zs = jnp.concatenate(zs_parts, axis=1)
        g = zs[:, GROUP:2 * GROUP] * zs[:, 2 * GROUP:]
        tail = stail[...]
        t = tsc_ref[...]
        conv = t[2:3, :] * g + t[1:2, :] * shifted(g, tail, 1) + t[0:1, :] * shifted(g, tail, 2)
        yb_out[pl.ds(r0, CF_ROWS), :] = (zs[:, :GROUP] * conv).astype(yb_out.dtype)
        stail[...] = g[CF_ROWS - SUBLANES:, :]
        return carry

    lax.fori_loop(0, tm // CF_ROWS, row_block, 0)
    buf[:, 0:CF_HALO, :] = buf[:, tm:tm + CF_HALO, :]


def _convmix(l, h, w_in, tsc, tcf, bcf, gln, bln):
    tm = ROW_TILE
    row = pl.BlockSpec((tm, GROUP), lambda i: (i, 0))
    return pl.pallas_call(
        _convmix_kernel,
        grid=(SEQ // tm,),
        in_specs=[pl.BlockSpec((tm, D_MODEL), lambda i: (i, 0)), _w_in_spec(l, "sconv"),
                  _w_in_spec(l, "conformer")] + [_lay(a, l) for a in (tsc, tcf, bcf, gln, bln)],
        out_specs=[row, row],
        out_shape=[jax.ShapeDtypeStruct((SEQ, GROUP), BF16), jax.ShapeDtypeStruct((SEQ, GROUP), BF16)],
        scratch_shapes=[pltpu.VMEM((SUBLANES, GROUP), F32),
                        pltpu.VMEM((GROUP // LANES, tm + CF_HALO, LANES), F32),
                        pltpu.VMEM((SUBLANES - 1, GROUP // LANES, tm + CF_HALO - SUBLANES, LANES), F32),
                        pltpu.VMEM((CF_ROWS, GROUP), F32)],
        compiler_params=_params(("arbitrary",)),
        name="convmix",
    )(h, w_in, w_in, tsc, tcf, bcf, gln, bln)


def _retention_kernel(h_ref, pos_ref, wqk_ref, wv_ref, wg_ref, freq_ref, qmask_ref, kdec_ref, qdec_ref,
                      inner_ref, cdec_ref, gret_ref, o_ref, state):
    tm = ROW_TILE
    c = RET_CHUNK
    dk2 = RET_HEADS * RET_DK
    h = h_ref[...]
    zqk = _rdot(h, wqk_ref[...])
    v = _rdot(h, wv_ref[...])
    gate = _rdot(h, wg_ref[...])

    @pl.when(pl.program_id(0) == 0)
    def _():
        state[...] = jnp.zeros(state.shape, F32)

    half = RET_DK // 2
    ang = pos_ref[...] * freq_ref[...]
    lane = lax.broadcasted_iota(jnp.int32, ang.shape, 1)
    first_half = (lane % RET_DK) < half
    cos_t = jnp.cos(ang)
    sin_t = jnp.sin(ang)
    sin_a = jnp.where(first_half, -sin_t, 0.0)
    sin_b = jnp.where(first_half, 0.0, sin_t)

    def rot(t):
        return jnp.concatenate(
            [x * cos_t + pltpu.roll(x, LANES - half, axis=1) * sin_a + pltpu.roll(x, half, axis=1) * sin_b
             for x in (t[:, :LANES], t[:, LANES:])], axis=1)

    q = rot(zqk[:, :dk2])
    k = rot(zqk[:, dk2:]) * (RET_DK ** -0.5)

    for ci in range(tm // c):
        rows = slice(ci * c, (ci + 1) * c)
        qc = q[rows]
        kt = k[rows].T
        kt_b = kt.astype(BF16)
        for hd in range(RET_HEADS):
            cols = slice(hd * RET_DV, (hd + 1) * RET_DV)
            qm = (qc * qmask_ref[hd]).astype(BF16)
            vh = v[rows, cols].astype(BF16)
            sc = _dot(qm, kt_b) * inner_ref[hd]
            st = state[hd]
            o = _dot(sc.astype(BF16), vh) + _dot(qm, st.astype(BF16)) * qdec_ref[hd]
            state[hd] = st * cdec_ref[hd] + _dot((kt * kdec_ref[hd]).astype(BF16), vh)
            on = o * _rms_scale(o, RET_DV) * gret_ref[hd]
            gh = gate[rows, cols]
            o_ref[rows, cols] = (gh * jax.nn.sigmoid(gh) * on).astype(o_ref.dtype)


def _retention(l, h, posf, w_in, freq, qmask, kdec, qdec, inner, cdec, gret):
    tm = ROW_TILE
    return pl.pallas_call(
        _retention_kernel,
        grid=(SEQ // tm,),
        in_specs=[pl.BlockSpec((tm, D_MODEL), lambda i: (i, 0)),
                  pl.BlockSpec((tm, 1), lambda i: (i, 0)), _w_in_spec(l, "ret_qk"),
                  _w_in_spec(l, "ret_v"), _w_in_spec(l, "ret_gate")]
        + [_const(a) for a in (freq, qmask, kdec, qdec, inner, cdec)] + [_lay(gret, l)],
        out_specs=pl.BlockSpec((tm, GROUP), lambda i: (i, 0)),
        out_shape=jax.ShapeDtypeStruct((SEQ, GROUP), BF16),
        scratch_shapes=[pltpu.VMEM((RET_HEADS, RET_HEADS * RET_DK, RET_DV), F32)],
        compiler_params=_params(("arbitrary",)),
        name="retention",
    )(h, posf, w_in, w_in, w_in, freq, qmask, kdec, qdec, inner, cdec, gret)


def _out_proj_kernel(x_ref, ya_ref, yb_ref, yc_ref, yd_ref, wo_ref, g_ref, o_ref, h_out):
    acc = x_ref[...]
    for n, y_ref in enumerate((ya_ref, yb_ref, yc_ref, yd_ref)):
        acc = acc + _dot(y_ref[...], wo_ref[n * GROUP:(n + 1) * GROUP, :])
    o_ref[...] = acc
    h_out[...] = _normed_bf16(acc, g_ref)


def _out_proj(l, x, ya, yb, yc, yd, wo, g_ffn):
    tm = ROW_TILE
    row = lambda n: pl.BlockSpec((tm, n), lambda i: (i, 0))
    return pl.pallas_call(
        _out_proj_kernel,
        grid=(SEQ // tm,),
        in_specs=[row(D_MODEL), row(GROUP), row(GROUP), row(GROUP), row(GROUP), _lay(wo, l),
                  _lay(g_ffn, l)],
        out_specs=[row(D_MODEL), row(D_MODEL)],
        out_shape=[jax.ShapeDtypeStruct((SEQ, D_MODEL), F32), jax.ShapeDtypeStruct((SEQ, D_MODEL), BF16)],
        compiler_params=_params(("parallel",)),
        name="out_proj",
    )(x, ya, yb, yc, yd, wo, g_ffn)


def _ffn_kernel(x_ref, h_ref, wg_ref, wu_ref, cg_ref, cu_ref, wd_ref, o_ref, ubuf, carry):
    tm = ROW_TILE
    i = pl.program_id(0)
    j = pl.program_id(1)

    @pl.when(j == 0)
    def _():
        o_ref[...] = x_ref[...]

    @pl.when(i == 0)
    def _():
        carry[j] = jnp.zeros(carry.shape[1:], F32)

    ts = FF_TILE // FF_SUB
    h = h_ref[...]

    def conv(part, s, u, c_ref):
        cols = slice(s * ts, (s + 1) * ts)
        ubuf[part, s, 0:SUBLANES, :] = carry[j, part, :, cols]
        ubuf[part, s, SUBLANES:, :] = u
        carry[j, part, :, cols] = u[tm - SUBLANES:, :]
        w = c_ref[:, cols]
        return (w[2:3, :] * u + w[1:2, :] * ubuf[part, s, SUBLANES - 1:SUBLANES - 1 + tm, :]
                + w[0:1, :] * ubuf[part, s, SUBLANES - 2:SUBLANES - 2 + tm, :])

    us = [(_rdot(h, wg_ref[:, s * ts:(s + 1) * ts]), _rdot(h, wu_ref[:, s * ts:(s + 1) * ts]))
          for s in range(FF_SUB)]
    for s, (ug, uu) in enumerate(us):
        gate = conv(0, s, ug, cg_ref)
        up = conv(1, s, uu, cu_ref)
        act = (gate * jax.nn.sigmoid(gate) * up).astype(BF16)
        o_ref[...] += _dot(act, wd_ref[s * ts:(s + 1) * ts, :])


def _ffn(l, x, h, w_up, w_conv, w_down):
    tm, tf = ROW_TILE, FF_TILE
    nf = D_FF // tf
    return pl.pallas_call(
        _ffn_kernel,
        grid=(SEQ // tm, nf),
        in_specs=[pl.BlockSpec((tm, D_MODEL), lambda i, j: (i, 0)),
                  pl.BlockSpec((tm, D_MODEL), lambda i, j: (i, 0)),
                  pl.BlockSpec((None, D_MODEL, tf), lambda i, j: (l, 0, j)),
                  pl.BlockSpec((None, D_MODEL, tf), lambda i, j: (l, 0, j + nf)),
                  pl.BlockSpec((None, FFN_K, tf), lambda i, j: (l, 0, j)),
                  pl.BlockSpec((None, FFN_K, tf), lambda i, j: (l, 0, j + nf)),
                  pl.BlockSpec((None, tf, D_MODEL), lambda i, j: (l, j, 0))],
        out_specs=pl.BlockSpec((tm, D_MODEL), lambda i, j: (i, 0)),
        out_shape=jax.ShapeDtypeStruct((SEQ, D_MODEL), F32),
        scratch_shapes=[pltpu.VMEM((2, FF_SUB, tm + SUBLANES, tf // FF_SUB), F32),
                        pltpu.VMEM((nf, 2, SUBLANES, tf), F32)],
        compiler_params=_params(("arbitrary", "arbitrary")),
        name="ffn",
    )(x, h, w_up, w_up, w_conv, w_conv, w_down)


def _ple_kernel(x_ref, p_ref, g_ref, wpg_ref, wpe_ref, *rest):
    o_ref = rest[-2] if len(rest) == 3 else rest[-1]
    hn = _normed_bf16(x_ref[...], g_ref)
    pb = p_ref[...].astype(BF16)
    for n in range(D_MODEL // GROUP):
        cols = slice(n * GROUP, (n + 1) * GROUP)
        gate = jax.nn.sigmoid(_dot(hn, wpg_ref[:, cols]))
        o_ref[:, cols] = x_ref[:, cols] + _dot(pb, wpe_ref[:, cols]) * gate
    if len(rest) == 3:
        gnext_ref, _, h_out = rest
        h_out[...] = _normed_bf16(o_ref[...], gnext_ref)


def _ple(l, x, p, g, wpg, wpe, g_mix):
    tm = ROW_TILE
    row = lambda n: pl.BlockSpec((tm, n), lambda i: (i, 0))
    last = l == DEPTH - 1
    in_specs = [row(D_MODEL), pl.BlockSpec((None, None, tm, PLE_DIM), lambda i: (l, 0, i, 0)),
                _lay(g, l), _lay(wpg, l), _lay(wpe, l)]
    out_specs = [row(D_MODEL)]
    out_shape = [jax.ShapeDtypeStruct((SEQ, D_MODEL), F32)]
    args = [x, p, g, wpg, wpe]
    if not last:
        in_specs.append(_lay(g_mix, l + 1))
        out_specs.append(row(D_MODEL))
        out_shape.append(jax.ShapeDtypeStruct((SEQ, D_MODEL), BF16))
        args.append(g_mix)
    outs = pl.pallas_call(
        _ple_kernel,
        grid=(SEQ // tm,),
        in_specs=in_specs,
        out_specs=out_specs,
        out_shape=out_shape,
        compiler_params=_params(("parallel",)),
        name="ple",
    )(*args)
    return (outs[0], None) if last else (outs[0], outs[1])


def _retention_tables():
    c = RET_CHUNK
    lg = jnp.log(1.0 - 2.0 ** (-5.0 - jnp.arange(RET_HEADS, dtype=F32)))
    idx = jnp.arange(c, dtype=F32)
    rel = idx[:, None] - idx[None, :]
    inner = jnp.where(rel[None] >= 0, jnp.exp(lg[:, None, None] * rel[None]), 0.0)
    qdec = jnp.exp(lg[:, None] * (idx + 1.0))[:, :, None]
    kdec_t = jnp.exp(lg[:, None] * (c - 1.0 - idx))
    cdec = jnp.exp(lg * c)
    lane = np.arange(RET_HEADS * RET_DK)
    head_of_lane = lane // RET_DK
    hmask = jnp.asarray(head_of_lane[None, :] == np.arange(RET_HEADS)[:, None], F32)
    qmask = hmask[:, None, :]
    kdec = hmask[:, :, None] * kdec_t[:, None, :]
    cdec_b = jnp.broadcast_to(cdec[:, None, None], (RET_HEADS, 1, RET_DV))
    return qmask, kdec, qdec, inner, cdec_b


def _lane_freq(inv):
    return jnp.tile(inv, LANES // inv.shape[0])[None, :]


def kernel(x, p, positions, g_mix, w_in, g_qa, g_kva, w_q_up, w_kv_up, g_qn, g_kn, w_sc, w_cf,
           b_cf, g_cf_ln, b_cf_ln, g_ret, w_o, g_ffn, w_up, w_ffn_conv, w_down, g_pe, w_pe, w_pg):
    assert x.shape == (1, SEQ, D_MODEL)
    xs = x[0]
    posf = positions[0].astype(F32)[:, None]
    rows = lambda a: a[:, None, :]

    mla_freq = _lane_freq(ROPE_BASE ** (-jnp.arange(0, MLA_ROPE, 2, dtype=F32) / MLA_ROPE))
    ret_freq = _lane_freq(1.0 / (10000.0 ** jnp.linspace(0.0, 1.0, RET_DK // 2, dtype=F32)))
    qmask, kdec, qdec, inner, cdec = _retention_tables()

    w_in_r = jnp.pad(w_in, ((0, 0), (0, 0), (W_IN_SHIFT, 0))).astype(BF16)
    wa = w_in[:, :, :MLA_IN].astype(BF16)
    wq = jnp.pad(w_q_up.reshape(DEPTH, Q_LORA, MLA_HEADS, MLA_QK),
                 ((0, 0), (0, 0), (0, 0), (0, HEAD_PAD - MLA_QK))
                 ).reshape(DEPTH, Q_LORA, MLA_HEADS * HEAD_PAD).astype(BF16)
    wkv4 = w_kv_up.reshape(DEPTH, KV_LORA, MLA_HEADS, MLA_NOPE + MLA_V)
    wkv = jnp.concatenate([wkv4[..., :MLA_NOPE].reshape(DEPTH, KV_LORA, -1),
                           wkv4[..., MLA_NOPE:].reshape(DEPTH, KV_LORA, -1)], axis=2).astype(BF16)
    gq = jnp.pad(g_qn * (MLA_QK ** -0.5 * LOG2_E), ((0, 0), (0, HEAD_PAD - MLA_QK)))
    gkr = jnp.pad(g_kn[:, MLA_NOPE:], ((0, 0), (0, LANES - MLA_ROPE)))
    w_o_b, w_up_b, w_down_b = w_o.astype(BF16), w_up.astype(BF16), w_down.astype(BF16)
    w_pg_b, w_pe_b = w_pg.astype(BF16), w_pe.astype(BF16)
    gm, g_ret4 = rows(g_mix), g_ret[:, :, None, :]

    hm = _prenorm(0, xs, gm)
    for l in range(DEPTH):
        qt, k, vt = _mla_pre(l, hm, posf, wa, rows(g_qa), rows(g_kva), wq, wkv, rows(gq),
                             rows(g_kn[:, :MLA_NOPE]), rows(gkr), mla_freq)
        y_a = _flash(qt, k, vt)
        y_b, y_c = _convmix(l, hm, w_in_r, w_sc, w_cf, rows(b_cf), rows(g_cf_ln), rows(b_cf_ln))
        y_d = _retention(l, hm, posf, w_in_r, ret_freq, qmask, kdec, qdec, inner, cdec, g_ret4)
        xs, hf = _out_proj(l, xs, y_a, y_b, y_c, y_d, w_o_b, rows(g_ffn))
        xs = _ffn(l, xs, hf, w_up_b, w_ffn_conv, w_down_b)
        xs, hm = _ple(l, xs, p, rows(g_pe), w_pg_b, w_pe_b, gm)
    return xs[None]
```

```python
import numpy as np
import jax
import jax.numpy as jnp
from jax import lax
from jax.experimental import pallas as pl
from jax.experimental.pallas import tpu as pltpu

F32 = jnp.float32
BF16 = jnp.bfloat16

D_MODEL = 2048
SEQ = 8192
DEPTH = 2
GROUP = 512
MLA_HEADS = 4
MLA_NOPE = 128
MLA_ROPE = 64
MLA_V = 128
MLA_QK = MLA_NOPE + MLA_ROPE
Q_LORA = 512
KV_LORA = 256
ROPE_BASE = 10000.0
SC_K = 3
CF_K = 31
RET_HEADS = 4
RET_DV = 128
RET_DK = 64
RET_CHUNK = 128
D_FF = 5632
FFN_K = 3
PLE_DIM = 256
EPS = 1e-6
N_IN = Q_LORA + KV_LORA + MLA_ROPE + 3 * GROUP + 2 * GROUP + 2 * RET_HEADS * (RET_DK + RET_DV)

LANES = 128
SUBLANES = 8
HEAD_PAD = 2 * LANES
ROW_TILE = 512
FF_TILE = 512
FF_SUB = 2
DOT_ROWS = 128
ATT_TILE = 512
ATT_HEADS_PER_STEP = 4
V_ONES = 16
CF_HALO = 32
CF_ROWS = 128
MLA_IN = Q_LORA + KV_LORA + LANES
W_IN_SHIFT = 3 * GROUP - (Q_LORA + KV_LORA + MLA_ROPE)
W_IN_GROUPS = {"sconv": (3 * GROUP, 1), "conformer": (2 * GROUP, 3), "ret_qk": (GROUP, 8),
               "ret_v": (GROUP, 9), "ret_gate": (GROUP, 10)}
VMEM_LIMIT = 56 * 1024 * 1024
MASK_VALUE = -1e30
LOG2_E = 1.4426950408889634


def _lay(a, l):
    return pl.BlockSpec((None,) + a.shape[1:], lambda *_: (l,) + (0,) * (a.ndim - 1))


def _const(a):
    return pl.BlockSpec(a.shape, lambda *_: (0,) * a.ndim)


def _w_in_spec(l, group):
    width, block = W_IN_GROUPS[group]
    return pl.BlockSpec((None, D_MODEL, width), lambda *_: (l, 0, block))


def _params(sem):
    return pltpu.CompilerParams(dimension_semantics=sem, vmem_limit_bytes=VMEM_LIMIT)


def _rms_scale(xf, n):
    return lax.rsqrt(jnp.sum(xf * xf, axis=-1, keepdims=True) * (1.0 / n) + EPS)


def _normed_bf16(xf, g_ref):
    return (xf * _rms_scale(xf, D_MODEL) * g_ref[...]).astype(BF16)


def _prenorm_kernel(x_ref, g_ref, h_out):
    h_out[...] = _normed_bf16(x_ref[...], g_ref)


def _prenorm(l, x, g):
    tm = ROW_TILE
    row = pl.BlockSpec((tm, D_MODEL), lambda i: (i, 0))
    return pl.pallas_call(
        _prenorm_kernel,
        grid=(SEQ // tm,),
        in_specs=[row, _lay(g, l)],
        out_specs=row,
        out_shape=jax.ShapeDtypeStruct((SEQ, D_MODEL), BF16),
        compiler_params=_params(("parallel",)),
        name="prenorm",
    )(x, g)


def _shift_cast_kernel(w_ref, o_ref):
    w = w_ref[...]
    o_ref[...] = jnp.concatenate([jnp.zeros((w.shape[0], W_IN_SHIFT), F32), w], axis=1).astype(BF16)


def _shift_cast(w_in):
    rows = 256
    n_in = w_in.shape[-1]
    return pl.pallas_call(
        _shift_cast_kernel,
        grid=(DEPTH, D_MODEL // rows),
        in_specs=[pl.BlockSpec((None, rows, n_in), lambda l, i: (l, i, 0))],
        out_specs=pl.BlockSpec((None, rows, n_in + W_IN_SHIFT), lambda l, i: (l, i, 0)),
        out_shape=jax.ShapeDtypeStruct((DEPTH, D_MODEL, n_in + W_IN_SHIFT), BF16),
        compiler_params=_params(("parallel", "parallel")),
        name="shift_cast",
    )(w_in)


def _with_casts(body, n_in, n_out, n_cast):
    def kernel_fn(*refs):
        ins, srcs = refs[:n_in], refs[n_in:n_in + n_cast]
        outs = refs[n_in + n_cast:n_in + n_cast + n_out]
        dsts = refs[n_in + n_cast + n_out:n_in + 2 * n_cast + n_out]
        for src, dst in zip(srcs, dsts):
            dst[...] = src[...].astype(BF16)
        body(*ins, *outs, *refs[n_in + 2 * n_cast + n_out:])
    return kernel_fn


def _cast_specs(arrays, steps):
    ins, outs, shapes = [], [], []
    for a in arrays:
        blk = a.shape[:-2] + (a.shape[-2] // steps, a.shape[-1])
        idx = lambda i, nd=a.ndim: (0,) * (nd - 2) + (i, 0)
        ins.append(pl.BlockSpec(blk, idx))
        outs.append(pl.BlockSpec(blk, idx))
        shapes.append(jax.ShapeDtypeStruct(a.shape, BF16))
    return ins, outs, shapes


def _dot(a, b):
    return jnp.dot(a, b, preferred_element_type=F32)


def _rdot(a, b):
    n = a.shape[0] // DOT_ROWS
    return jnp.concatenate([_dot(a[r * DOT_ROWS:(r + 1) * DOT_ROWS], b) for r in range(n)], axis=0)


def _rope_tables_kernel(pos_ref, fm_ref, fr_ref, o_ref):
    lane = lax.broadcasted_iota(jnp.int32, (pos_ref.shape[0], LANES), 1)
    pos = pos_ref[...]
    for n, (f_ref, is_x1, is_x2) in enumerate((
            (fm_ref, lane < MLA_ROPE // 2, (lane >= MLA_ROPE // 2) & (lane < MLA_ROPE)),
            (fr_ref, (lane % RET_DK) < RET_DK // 2, (lane % RET_DK) >= RET_DK // 2))):
        ang = pos * f_ref[...]
        sin_t = jnp.sin(ang)
        o_ref[:, (3 * n) * LANES:(3 * n + 1) * LANES] = jnp.cos(ang)
        o_ref[:, (3 * n + 1) * LANES:(3 * n + 2) * LANES] = jnp.where(is_x1, -sin_t, 0.0)
        o_ref[:, (3 * n + 2) * LANES:(3 * n + 3) * LANES] = jnp.where(is_x2, sin_t, 0.0)


def _rope_tables(posf, mla_freq, ret_freq):
    tm = ROW_TILE
    return pl.pallas_call(
        _rope_tables_kernel,
        grid=(SEQ // tm,),
        in_specs=[pl.BlockSpec((tm, 1), lambda i: (i, 0)), _const(mla_freq), _const(ret_freq)],
        out_specs=pl.BlockSpec((tm, 6 * LANES), lambda i: (i, 0)),
        out_shape=jax.ShapeDtypeStruct((SEQ, 6 * LANES), F32),
        compiler_params=_params(("parallel",)),
        name="rope_tables",
    )(posf, mla_freq, ret_freq)


def _rope_spec(which):
    return pl.BlockSpec((ROW_TILE, 3 * LANES), lambda i: (i, which))


def _rope_slab(x, cos_t, sin_a, sin_b):
    return (x * cos_t + pltpu.roll(x, LANES - 32, axis=1) * sin_a
            + pltpu.roll(x, 32, axis=1) * sin_b)


def _mla_pre_kernel(h_ref, rope_ref, wa_ref, gqa_ref, gkva_ref, wq_ref, wkv_ref,
                    gq_ref, gkn_ref, gkr_ref, *rest):
    if len(rest) == 3:
        qt_out, k_out, vt_out = rest
    else:
        wup_ref, wdn_ref, qt_out, k_out, vt_out, wg_out, wu_out, wd_out = rest
        wg_out[...] = wup_ref[:, :D_FF].astype(BF16)
        wu_out[...] = wup_ref[:, D_FF:].astype(BF16)
        wd_out[...] = wdn_ref[...].astype(BF16)
    z = _rdot(h_ref[...], wa_ref[...])
    zq = z[:, :Q_LORA]
    zkv = z[:, Q_LORA:Q_LORA + KV_LORA]
    lane = lax.broadcasted_iota(jnp.int32, (z.shape[0], LANES), 1)
    zkr = jnp.where(lane < MLA_ROPE, z[:, Q_LORA + KV_LORA:], 0.0)
    c_q = (zq * _rms_scale(zq, Q_LORA) * gqa_ref[...]).astype(BF16)
    c_kv = (zkv * _rms_scale(zkv, KV_LORA) * gkva_ref[...]).astype(BF16)

    cos_t, sin_a, sin_b = (rope_ref[:, n * LANES:(n + 1) * LANES] for n in range(3))

    q = _rdot(c_q, wq_ref[...])
    kv = _rdot(c_kv, wkv_ref[...])

    kr = zkr * gkr_ref[...]
    ss_kr = jnp.sum(zkr * zkr, axis=-1, keepdims=True)
    kr_rot = _rope_slab(kr, cos_t, sin_a, sin_b)
    ones = jnp.ones((V_ONES, z.shape[0]), BF16)
    for hd in range(MLA_HEADS):
        qb = q[:, hd * HEAD_PAD:(hd + 1) * HEAD_PAD]
        qn = qb * _rms_scale(qb, MLA_QK) * gq_ref[...]
        qt_out[hd, 0:LANES, :] = qn[:, :LANES].T.astype(BF16)
        qt_out[hd, LANES:, :] = _rope_slab(qn[:, LANES:], cos_t, sin_a, sin_b).T.astype(BF16)
        vh = kv[:, MLA_HEADS * MLA_NOPE + hd * MLA_V:MLA_HEADS * MLA_NOPE + (hd + 1) * MLA_V]
        vt_out[hd, 0, 0:MLA_V, :] = vh.T.astype(BF16)
        vt_out[hd, 0, MLA_V:, :] = ones
        kn = kv[:, hd * MLA_NOPE:(hd + 1) * MLA_NOPE]
        r_k = lax.rsqrt((jnp.sum(kn * kn, axis=-1, keepdims=True) + ss_kr) * (1.0 / MLA_QK) + EPS)
        k_out[:, hd * HEAD_PAD:hd * HEAD_PAD + LANES] = (kn * r_k * gkn_ref[...]).astype(BF16)
        k_out[:, hd * HEAD_PAD + LANES:(hd + 1) * HEAD_PAD] = (kr_rot * r_k).astype(BF16)


def _mla_pre(l, h, rope, wa, g_qa, g_kva, wq, wkv, gq, gkn, gkr, w_up_f32=None, w_down_f32=None):
    tm = ROW_TILE
    n = SEQ // tm
    row = lambda w: pl.BlockSpec((tm, w), lambda i: (i, 0))
    in_specs = ([row(D_MODEL), _rope_spec(0)]
                + [_lay(a, l) for a in (wa, g_qa, g_kva, wq, wkv, gq, gkn, gkr)])
    out_specs = [pl.BlockSpec((MLA_HEADS, HEAD_PAD, tm), lambda i: (0, 0, i)),
                 row(MLA_HEADS * HEAD_PAD),
                 pl.BlockSpec((MLA_HEADS, 1, MLA_V + V_ONES, tm), lambda i: (0, i, 0, 0))]
    out_shape = [jax.ShapeDtypeStruct((MLA_HEADS, HEAD_PAD, SEQ), BF16),
                 jax.ShapeDtypeStruct((SEQ, MLA_HEADS * HEAD_PAD), BF16),
                 jax.ShapeDtypeStruct((MLA_HEADS, SEQ // tm, MLA_V + V_ONES, tm), BF16)]
    args = [h, rope, wa, g_qa, g_kva, wq, wkv, gq, gkn, gkr]
    if w_up_f32 is not None:
        ur, dr = D_MODEL // n, D_FF // n
        in_specs += [pl.BlockSpec((None, ur, 2 * D_FF), lambda i: (l, i, 0)),
                     pl.BlockSpec((None, dr, D_MODEL), lambda i: (l, i, 0))]
        out_specs += [pl.BlockSpec((ur, D_FF), lambda i: (i, 0)), pl.BlockSpec((ur, D_FF), lambda i: (i, 0)),
                      pl.BlockSpec((dr, D_MODEL), lambda i: (i, 0))]
        out_shape += [jax.ShapeDtypeStruct((D_MODEL, D_FF), BF16), jax.ShapeDtypeStruct((D_MODEL, D_FF), BF16),
                      jax.ShapeDtypeStruct((D_FF, D_MODEL), BF16)]
        args += [w_up_f32, w_down_f32]
    return pl.pallas_call(
        _mla_pre_kernel,
        grid=(n,),
        in_specs=in_specs,
        out_specs=out_specs,
        out_shape=out_shape,
        compiler_params=_params(("parallel",)),
        name="mla_pre",
    )(*args)


def _flash_kernel(qt_ref, k_ref, vt_ref, o_ref, m_scr, acc_scr):
    t = ATT_TILE
    i = pl.program_id(1)
    sub = SUBLANES
    pack = 2 * SUBLANES
    vrows = MLA_V + V_ONES
    m_scr[...] = jnp.full(m_scr.shape, MASK_VALUE, F32)
    acc_scr[...] = jnp.zeros(acc_scr.shape, F32)

    def block(j, masked):
        start = pl.multiple_of(j * t, t)
        if masked:
            key = lax.broadcasted_iota(jnp.int32, (t, t), 0)
            qry = lax.broadcasted_iota(jnp.int32, (t, t), 1)
            keep = key <= qry
        sts = [_rdot(k_ref[pl.ds(start, t), hh * HEAD_PAD:(hh + 1) * HEAD_PAD], qt_ref[hh])
               for hh in range(ATT_HEADS_PER_STEP)]
        for hh, st in enumerate(sts):
            if masked:
                st = jnp.where(keep, st, MASK_VALUE)
            s3 = st.reshape(t // pack, pack, t)
            mx = jnp.max(s3, axis=0)
            mx = jnp.maximum(mx[:sub], mx[sub:])
            for shift in (4, 2, 1):
                mx = jnp.maximum(mx, pltpu.roll(mx, shift, axis=0))
            m_prev = m_scr[hh]
            m_new = jnp.maximum(m_prev, mx)
            alpha = jnp.exp2(m_prev - m_new)
            m2 = jnp.concatenate([m_new, m_new], axis=0)
            p = jnp.exp2((s3 - m2[None]).astype(BF16)).reshape(t, t)
            pv = _dot(vt_ref[hh, j], p)
            acc = acc_scr[hh].reshape(vrows // sub, sub, t) * alpha[None] + pv.reshape(vrows // sub, sub, t)
            acc_scr[hh] = acc.reshape(vrows, t)
            m_scr[hh] = m_new

    def body(j, carry):
        block(j, False)
        return carry

    lax.fori_loop(0, i, body, 0)
    block(i, True)
    for hh in range(ATT_HEADS_PER_STEP):
        num = acc_scr[hh, 0:MLA_V, :].reshape(MLA_V // sub, sub, t)
        den = acc_scr[hh, MLA_V:MLA_V + sub, :]
        out_t = (num / den[None]).reshape(MLA_V, t)
        o_ref[:, hh * MLA_V:(hh + 1) * MLA_V] = out_t.T.astype(o_ref.dtype)


def _flash(qt, k, vt):
    t, hp = ATT_TILE, ATT_HEADS_PER_STEP
    vrows = MLA_V + V_ONES
    return pl.pallas_call(
        _flash_kernel,
        grid=(MLA_HEADS // hp, SEQ // t),
        in_specs=[pl.BlockSpec((hp, HEAD_PAD, t), lambda h, i: (h, 0, i)),
                  pl.BlockSpec((SEQ, hp * HEAD_PAD), lambda h, i: (0, h), pipeline_mode=pl.Buffered(1)),
                  pl.BlockSpec((hp, SEQ // t, vrows, t), lambda h, i: (h, 0, 0, 0),
                               pipeline_mode=pl.Buffered(1))],
        out_specs=pl.BlockSpec((t, hp * MLA_V), lambda h, i: (i, h)),
        out_shape=jax.ShapeDtypeStruct((SEQ, MLA_HEADS * MLA_V), BF16),
        scratch_shapes=[pltpu.VMEM((hp, SUBLANES, t), F32), pltpu.VMEM((hp, vrows, t), F32)],
        compiler_params=_params(("parallel", "arbitrary")),
        name="flash",
    )(qt, k, vt)


def _sconv_kernel(h_ref, w_ref, wsc_ref, o_ref, buf):
    tm = ROW_TILE
    z = _rdot(h_ref[...], w_ref[...])

    @pl.when(pl.program_id(0) == 0)
    def _():
        buf[0:SUBLANES, :] = jnp.zeros((SUBLANES, GROUP), F32)

    g = z[:, GROUP:2 * GROUP] * z[:, 2 * GROUP:]
    buf[SUBLANES:, :] = g
    w = wsc_ref[...]
    conv = (w[2:3, :] * g + w[1:2, :] * buf[SUBLANES - 1:SUBLANES - 1 + tm, :]
            + w[0:1, :] * buf[SUBLANES - 2:SUBLANES - 2 + tm, :])
    o_ref[...] = (z[:, :GROUP] * conv).astype(o_ref.dtype)
    buf[0:SUBLANES, :] = buf[tm:tm + SUBLANES, :]


def _sconv(l, h, w_in, wsc, cast=()):
    tm = ROW_TILE
    n = SEQ // tm
    c_in, c_out, c_shape = _cast_specs(cast, n)
    return pl.pallas_call(
        _with_casts(_sconv_kernel, 3, 1, len(cast)),
        grid=(n,),
        in_specs=[pl.BlockSpec((tm, D_MODEL), lambda i: (i, 0)),
                  _w_in_spec(l, "sconv"), _lay(wsc, l)] + c_in,
        out_specs=[pl.BlockSpec((tm, GROUP), lambda i: (i, 0))] + c_out,
        out_shape=[jax.ShapeDtypeStruct((SEQ, GROUP), BF16)] + c_shape,
        scratch_shapes=[pltpu.VMEM((tm + SUBLANES, GROUP), F32)],
        compiler_params=_params(("arbitrary",)),
        name="sconv",
    )(h, w_in, wsc, *cast)


def _conformer_kernel(h_ref, w_ref, wcf_ref, bcf_ref, gln_ref, bln_ref, o_ref, buf, wbuf, uscr):
    tm = ROW_TILE
    nslab = GROUP // LANES
    z = _rdot(h_ref[...], w_ref[...])

    @pl.when(pl.program_id(0) == 0)
    def _():
        buf[:, 0:CF_HALO, :] = jnp.zeros((nslab, CF_HALO, LANES), F32)

    glu = z[:, :GROUP] * jax.nn.sigmoid(z[:, GROUP:])
    for sl in range(nslab):
        buf[sl, CF_HALO:, :] = glu[:, sl * LANES:(sl + 1) * LANES]
    base = CF_HALO - (CF_K - 1)
    for phase in range(1, SUBLANES):
        wbuf[phase - 1] = buf[:, phase:phase + tm + CF_HALO - SUBLANES, :]

    def tap_rows(o, r0, sl):
        phase = o % SUBLANES
        start = pl.multiple_of(r0 + (o - phase), SUBLANES)
        if phase == 0:
            return buf[sl, pl.ds(start, CF_ROWS), :]
        return wbuf[phase - 1, sl, pl.ds(start, CF_ROWS), :]

    def row_block(rb, carry):
        r0 = pl.multiple_of(rb * CF_ROWS, CF_ROWS)
        for sl in range(nslab):
            lanes = slice(sl * LANES, (sl + 1) * LANES)
            acc = jnp.broadcast_to(bcf_ref[:, lanes], (CF_ROWS, LANES))
            for j in range(CF_K):
                acc = acc + wcf_ref[j:j + 1, lanes] * tap_rows(base + j, r0, sl)
            uscr[pl.ds(r0, CF_ROWS), lanes] = acc
        return carry

    lax.fori_loop(0, tm // CF_ROWS, row_block, 0)
    u = uscr[...]
    mu = jnp.mean(u, axis=-1, keepdims=True)
    uc = u - mu
    y = uc * lax.rsqrt(jnp.mean(uc * uc, axis=-1, keepdims=True) + EPS)
    y = y * gln_ref[...] + bln_ref[...]
    o_ref[...] = (y * jax.nn.sigmoid(y)).astype(o_ref.dtype)
    buf[:, 0:CF_HALO, :] = buf[:, tm:tm + CF_HALO, :]


def _conformer(l, h, w_in, wcf, bcf, gln, bln, cast=()):
    tm = ROW_TILE
    n = SEQ // tm
    c_in, c_out, c_shape = _cast_specs(cast, n)
    return pl.pallas_call(
        _with_casts(_conformer_kernel, 6, 1, len(cast)),
        grid=(n,),
        in_specs=[pl.BlockSpec((tm, D_MODEL), lambda i: (i, 0)), _w_in_spec(l, "conformer")]
        + [_lay(a, l) for a in (wcf, bcf, gln, bln)] + c_in,
        out_specs=[pl.BlockSpec((tm, GROUP), lambda i: (i, 0))] + c_out,
        out_shape=[jax.ShapeDtypeStruct((SEQ, GROUP), BF16)] + c_shape,
        scratch_shapes=[pltpu.VMEM((GROUP // LANES, tm + CF_HALO, LANES), F32),
                        pltpu.VMEM((SUBLANES - 1, GROUP // LANES, tm + CF_HALO - SUBLANES, LANES), F32),
                        pltpu.VMEM((tm, GROUP), F32)],
        compiler_params=_params(("arbitrary",)),
        name="conformer",
    )(h, w_in, wcf, bcf, gln, bln, *cast)


def _retention_kernel(h_ref, rope_ref, wqk_ref, wv_ref, wg_ref, qmask_ref, kdec_ref, qdec_ref,
                      inner_ref, cdec_ref, gret_ref, o_ref, state):
    tm = ROW_TILE
    c = RET_CHUNK
    dk2 = RET_HEADS * RET_DK
    h = h_ref[...]
    zqk = _rdot(h, wqk_ref[...])
    v = _rdot(h, wv_ref[...])
    gate = _rdot(h, wg_ref[...])

    @pl.when(pl.program_id(0) == 0)
    def _():
        state[...] = jnp.zeros(state.shape, F32)

    half = RET_DK // 2
    cos_t, sin_a, sin_b = (rope_ref[:, n * LANES:(n + 1) * LANES] for n in range(3))

    def rot(t):
        return jnp.concatenate(
            [x * cos_t + pltpu.roll(x, LANES - half, axis=1) * sin_a + pltpu.roll(x, half, axis=1) * sin_b
             for x in (t[:, :LANES], t[:, LANES:])], axis=1)

    q = rot(zqk[:, :dk2])
    k = rot(zqk[:, dk2:]) * (RET_DK ** -0.5)

    for ci in range(tm // c):
        rows = slice(ci * c, (ci + 1) * c)
        qc = q[rows]
        kt = k[rows].T
        kt_b = kt.astype(BF16)
        for hd in range(RET_HEADS):
            cols = slice(hd * RET_DV, (hd + 1) * RET_DV)
            qm = (qc * qmask_ref[hd]).astype(BF16)
            vh = v[rows, cols].astype(BF16)
            sc = _dot(qm, kt_b) * inner_ref[hd]
            st = state[hd]
            o = _dot(sc.astype(BF16), vh) + _dot(qm, st.astype(BF16)) * qdec_ref[hd]
            state[hd] = st * cdec_ref[hd] + _dot((kt * kdec_ref[hd]).astype(BF16), vh)
            on = o * _rms_scale(o, RET_DV) * gret_ref[hd]
            gh = gate[rows, cols]
            o_ref[rows, cols] = (gh * jax.nn.sigmoid(gh) * on).astype(o_ref.dtype)


def _retention(l, h, rope, w_in, qmask, kdec, qdec, inner, cdec, gret):
    tm = ROW_TILE
    return pl.pallas_call(
        _retention_kernel,
        grid=(SEQ // tm,),
        in_specs=[pl.BlockSpec((tm, D_MODEL), lambda i: (i, 0)), _rope_spec(1), _w_in_spec(l, "ret_qk"),
                  _w_in_spec(l, "ret_v"), _w_in_spec(l, "ret_gate")]
        + [_const(a) for a in (qmask, kdec, qdec, inner, cdec)] + [_lay(gret, l)],
        out_specs=pl.BlockSpec((tm, GROUP), lambda i: (i, 0)),
        out_shape=jax.ShapeDtypeStruct((SEQ, GROUP), BF16),
        scratch_shapes=[pltpu.VMEM((RET_HEADS, RET_HEADS * RET_DK, RET_DV), F32)],
        compiler_params=_params(("arbitrary",)),
        name="retention",
    )(h, rope, w_in, w_in, w_in, qmask, kdec, qdec, inner, cdec, gret)


def _out_proj_kernel(x_ref, ya_ref, yb_ref, yc_ref, yd_ref, wo_ref, g_ref, o_ref, h_out):
    acc = x_ref[...]
    for n, y_ref in enumerate((ya_ref, yb_ref, yc_ref, yd_ref)):
        acc = acc + _dot(y_ref[...], wo_ref[n * GROUP:(n + 1) * GROUP, :])
    o_ref[...] = acc
    h_out[...] = _normed_bf16(acc, g_ref)


def _out_proj(l, x, ya, yb, yc, yd, wo, g_ffn):
    tm = ROW_TILE
    row = lambda n: pl.BlockSpec((tm, n), lambda i: (i, 0))
    return pl.pallas_call(
        _out_proj_kernel,
        grid=(SEQ // tm,),
        in_specs=[row(D_MODEL), row(GROUP), row(GROUP), row(GROUP), row(GROUP), _lay(wo, l),
                  _lay(g_ffn, l)],
        out_specs=[row(D_MODEL), row(D_MODEL)],
        out_shape=[jax.ShapeDtypeStruct((SEQ, D_MODEL), F32), jax.ShapeDtypeStruct((SEQ, D_MODEL), BF16)],
        compiler_params=_params(("parallel",)),
        name="out_proj",
    )(x, ya, yb, yc, yd, wo, g_ffn)


def _ffn_kernel(x_ref, h_ref, wg_ref, wu_ref, cg_ref, cu_ref, wd_ref, *rest):
    tm = ROW_TILE
    i = pl.program_id(0)
    j = pl.program_id(1)
    if len(rest) == 3:
        o_ref, ubuf, carry = rest
    else:
        nxt_in, (o_ref, *nxt_out), (ubuf, carry) = rest[:3], rest[3:7], rest[7:]
        for src, dst in zip(nxt_in, nxt_out):
            dst[...] = src[...].astype(BF16)

    @pl.when(j == 0)
    def _():
        o_ref[...] = x_ref[...]

    @pl.when(i == 0)
    def _():
        carry[j] = jnp.zeros(carry.shape[1:], F32)

    ts = FF_TILE // FF_SUB
    h = h_ref[...]

    def conv(part, s, u, c_ref):
        cols = slice(s * ts, (s + 1) * ts)
        ubuf[part, s, 0:SUBLANES, :] = carry[j, part, :, cols]
        ubuf[part, s, SUBLANES:, :] = u
        carry[j, part, :, cols] = u[tm - SUBLANES:, :]
        w = c_ref[:, cols]
        return (w[2:3, :] * u + w[1:2, :] * ubuf[part, s, SUBLANES - 1:SUBLANES - 1 + tm, :]
                + w[0:1, :] * ubuf[part, s, SUBLANES - 2:SUBLANES - 2 + tm, :])

    us = [(_rdot(h, wg_ref[:, s * ts:(s + 1) * ts]), _rdot(h, wu_ref[:, s * ts:(s + 1) * ts]))
          for s in range(FF_SUB)]
    for s, (ug, uu) in enumerate(us):
        gate = conv(0, s, ug, cg_ref)
        up = conv(1, s, uu, cu_ref)
        act = (gate * jax.nn.sigmoid(gate) * up).astype(BF16)
        o_ref[...] += _dot(act, wd_ref[s * ts:(s + 1) * ts, :])


def _ffn(l, x, h, wg, wu, wd, w_conv, w_up_f32, w_down_f32):
    tm, tf = ROW_TILE, FF_TILE
    nf, ni = D_FF // tf, SEQ // tm
    in_specs = [pl.BlockSpec((tm, D_MODEL), lambda i, j: (i, 0)),
                pl.BlockSpec((tm, D_MODEL), lambda i, j: (i, 0)),
                pl.BlockSpec((D_MODEL, tf), lambda i, j: (0, j)),
                pl.BlockSpec((D_MODEL, tf), lambda i, j: (0, j)),
                pl.BlockSpec((None, FFN_K, tf), lambda i, j: (l, 0, j)),
                pl.BlockSpec((None, FFN_K, tf), lambda i, j: (l, 0, j + nf)),
                pl.BlockSpec((tf, D_MODEL), lambda i, j: (j, 0))]
    out_specs = [pl.BlockSpec((tm, D_MODEL), lambda i, j: (i, 0))]
    out_shape = [jax.ShapeDtypeStruct((SEQ, D_MODEL), F32)]
    args = [x, h, wg, wu, w_conv, w_conv, wd]
    last = l == DEPTH - 1
    if not last:
        ur, dc = D_MODEL // ni, D_MODEL // ni
        in_specs += [pl.BlockSpec((None, ur, tf), lambda i, j: (l + 1, i, j)),
                     pl.BlockSpec((None, ur, tf), lambda i, j: (l + 1, i, j + nf)),
                     pl.BlockSpec((None, tf, dc), lambda i, j: (l + 1, j, i))]
        out_specs += [pl.BlockSpec((ur, tf), lambda i, j: (i, j)),
                      pl.BlockSpec((ur, tf), lambda i, j: (i, j)),
                      pl.BlockSpec((tf, dc), lambda i, j: (j, i))]
        out_shape += [jax.ShapeDtypeStruct((D_MODEL, D_FF), BF16), jax.ShapeDtypeStruct((D_MODEL, D_FF), BF16),
                      jax.ShapeDtypeStruct((D_FF, D_MODEL), BF16)]
        args += [w_up_f32, w_up_f32, w_down_f32]
    outs = pl.pallas_call(
        _ffn_kernel,
        grid=(ni, nf),
        in_specs=in_specs,
        out_specs=out_specs,
        out_shape=out_shape,
        scratch_shapes=[pltpu.VMEM((2, FF_SUB, tm + SUBLANES, tf // FF_SUB), F32),
                        pltpu.VMEM((nf, 2, SUBLANES, tf), F32)],
        compiler_params=_params(("arbitrary", "arbitrary")),
        name="ffn",
    )(*args)
    return (outs[0], None, None, None) if last else tuple(outs)


def _ple_kernel(x_ref, p_ref, g_ref, wpg_ref, wpe_ref, *rest):
    o_ref = rest[-2] if len(rest) == 3 else rest[-1]
    hn = _normed_bf16(x_ref[...], g_ref)
    pb = p_ref[...].astype(BF16)
    for n in range(D_MODEL // GROUP):
        cols = slice(n * GROUP, (n + 1) * GROUP)
        gate = jax.nn.sigmoid(_dot(hn, wpg_ref[:, cols]))
        o_ref[:, cols] = x_ref[:, cols] + _dot(pb, wpe_ref[:, cols]) * gate
    if len(rest) == 3:
        gnext_ref, _, h_out = rest
        h_out[...] = _normed_bf16(o_ref[...], gnext_ref)


def _ple(l, x, p, g, wpg, wpe, g_mix):
    tm = ROW_TILE
    row = lambda n: pl.BlockSpec((tm, n), lambda i: (i, 0))
    last = l == DEPTH - 1
    in_specs = [row(D_MODEL), pl.BlockSpec((None, None, tm, PLE_DIM), lambda i: (l, 0, i, 0)),
                _lay(g, l), _lay(wpg, l), _lay(wpe, l)]
    out_specs = [row(D_MODEL)]
    out_shape = [jax.ShapeDtypeStruct((SEQ, D_MODEL), F32)]
    args = [x, p, g, wpg, wpe]
    if not last:
        in_specs.append(_lay(g_mix, l + 1))
        out_specs.append(row(D_MODEL))
        out_shape.append(jax.ShapeDtypeStruct((SEQ, D_MODEL), BF16))
        args.append(g_mix)
    outs = pl.pallas_call(
        _ple_kernel,
        grid=(SEQ // tm,),
        in_specs=in_specs,
        out_specs=out_specs,
        out_shape=out_shape,
        compiler_params=_params(("parallel",)),
        name="ple",
    )(*args)
    return (outs[0], None) if last else (outs[0], outs[1])


def _retention_tables():
    c = RET_CHUNK
    lg = jnp.log(1.0 - 2.0 ** (-5.0 - jnp.arange(RET_HEADS, dtype=F32)))
    idx = jnp.arange(c, dtype=F32)
    rel = idx[:, None] - idx[None, :]
    inner = jnp.where(rel[None] >= 0, jnp.exp(lg[:, None, None] * rel[None]), 0.0)
    qdec = jnp.exp(lg[:, None] * (idx + 1.0))[:, :, None]
    kdec_t = jnp.exp(lg[:, None] * (c - 1.0 - idx))
    cdec = jnp.exp(lg * c)
    lane = np.arange(RET_HEADS * RET_DK)
    head_of_lane = lane // RET_DK
    hmask = jnp.asarray(head_of_lane[None, :] == np.arange(RET_HEADS)[:, None], F32)
    qmask = hmask[:, None, :]
    kdec = hmask[:, :, None] * kdec_t[:, None, :]
    cdec_b = jnp.broadcast_to(cdec[:, None, None], (RET_HEADS, 1, RET_DV))
    return qmask, kdec, qdec, inner, cdec_b


def _lane_freq(inv):
    return jnp.tile(inv, LANES // inv.shape[0])[None, :]


def kernel(x, p, positions, g_mix, w_in, g_qa, g_kva, w_q_up, w_kv_up, g_qn, g_kn, w_sc, w_cf,
           b_cf, g_cf_ln, b_cf_ln, g_ret, w_o, g_ffn, w_up, w_ffn_conv, w_down, g_pe, w_pe, w_pg):
    assert x.shape == (1, SEQ, D_MODEL)
    xs = x[0]
    posf = positions[0].astype(F32)[:, None]
    rows = lambda a: a[:, None, :]

    mla_freq = _lane_freq(ROPE_BASE ** (-jnp.arange(0, MLA_ROPE, 2, dtype=F32) / MLA_ROPE))
    ret_freq = _lane_freq(1.0 / (10000.0 ** jnp.linspace(0.0, 1.0, RET_DK // 2, dtype=F32)))
    qmask, kdec, qdec, inner, cdec = _retention_tables()

    w_in_r = _shift_cast(w_in)
    wa = w_in[:, :, :MLA_IN].astype(BF16)
    wq = jnp.pad(w_q_up.reshape(DEPTH, Q_LORA, MLA_HEADS, MLA_QK),
                 ((0, 0), (0, 0), (0, 0), (0, HEAD_PAD - MLA_QK))
                 ).reshape(DEPTH, Q_LORA, MLA_HEADS * HEAD_PAD).astype(BF16)
    wkv4 = w_kv_up.reshape(DEPTH, KV_LORA, MLA_HEADS, MLA_NOPE + MLA_V)
    wkv = jnp.concatenate([wkv4[..., :MLA_NOPE].reshape(DEPTH, KV_LORA, -1),
                           wkv4[..., MLA_NOPE:].reshape(DEPTH, KV_LORA, -1)], axis=2).astype(BF16)
    gq = jnp.pad(g_qn * (MLA_QK ** -0.5 * LOG2_E), ((0, 0), (0, HEAD_PAD - MLA_QK)))
    gkr = jnp.pad(g_kn[:, MLA_NOPE:], ((0, 0), (0, LANES - MLA_ROPE)))
    gm, g_ret4 = rows(g_mix), g_ret[:, :, None, :]

    rope = _rope_tables(posf, mla_freq, ret_freq)
    hm = _prenorm(0, xs, gm)
    for l in range(DEPTH):
        mla = _mla_pre(l, hm, rope, wa, rows(g_qa), rows(g_kva), wq, wkv, rows(gq),
                       rows(g_kn[:, :MLA_NOPE]), rows(gkr), *((w_up, w_down) if l == 0 else ()))
        qt, k, vt = mla[:3]
        if l == 0:
            wg, wu, wd = mla[3:]
        y_a = _flash(qt, k, vt)
        first = l == 0
        y_b, *cast_b = _sconv(l, hm, w_in_r, w_sc, cast=(w_o,) if first else ())
        y_c, *cast_c = _conformer(l, hm, w_in_r, w_cf, rows(b_cf), rows(g_cf_ln), rows(b_cf_ln),
                                  cast=(w_pg, w_pe) if first else ())
        if first:
            (w_o_b,), (w_pg_b, w_pe_b) = cast_b, cast_c
        y_d = _retention(l, hm, rope, w_in_r, qmask, kdec, qdec, inner, cdec, g_ret4)
        xs, hf = _out_proj(l, xs, y_a, y_b, y_c, y_d, w_o_b, rows(g_ffn))
        xs, wg, wu, wd = _ffn(l, xs, hf, wg, wu, wd, w_ffn_conv, w_up, w_down)
        xs, hm = _ple(l, xs, p, rows(g_pe), w_pg_b, w_pe_b, gm)
    return xs[None]
```

```python
import numpy as np
import jax
import jax.numpy as jnp
from jax import lax
from jax.experimental import pallas as pl
from jax.experimental.pallas import tpu as pltpu

F32 = jnp.float32
BF16 = jnp.bfloat16

D_MODEL = 2048
SEQ = 8192
DEPTH = 2
GROUP = 512
MLA_HEADS = 4
MLA_NOPE = 128
MLA_ROPE = 64
MLA_V = 128
MLA_QK = MLA_NOPE + MLA_ROPE
Q_LORA = 512
KV_LORA = 256
ROPE_BASE = 10000.0
SC_K = 3
CF_K = 31
RET_HEADS = 4
RET_DV = 128
RET_DK = 64
RET_CHUNK = 128
D_FF = 5632
FFN_K = 3
PLE_DIM = 256
EPS = 1e-6
N_IN = Q_LORA + KV_LORA + MLA_ROPE + 3 * GROUP + 2 * GROUP + 2 * RET_HEADS * (RET_DK + RET_DV)

LANES = 128
SUBLANES = 8
HEAD_PAD = 2 * LANES
ROW_TILE = 512
FF_TILE = 512
FF_SUB = 2
DOT_ROWS = 128
ATT_TILE = 512
ATT_HEADS_PER_STEP = 4
V_ONES = 16
CF_HALO = 32
CF_ROWS = 128
MLA_IN = Q_LORA + KV_LORA + LANES
W_IN_SHIFT = 3 * GROUP - (Q_LORA + KV_LORA + MLA_ROPE)
W_IN_GROUPS = {"sconv": (3 * GROUP, 1), "conformer": (2 * GROUP, 3), "ret_qk": (GROUP, 8),
               "ret_v": (GROUP, 9), "ret_gate": (GROUP, 10)}
VMEM_LIMIT = 56 * 1024 * 1024
MASK_VALUE = -1e30
LOG2_E = 1.4426950408889634


def _lay(a, l):
    return pl.BlockSpec((None,) + a.shape[1:], lambda *_: (l,) + (0,) * (a.ndim - 1))


def _const(a):
    return pl.BlockSpec(a.shape, lambda *_: (0,) * a.ndim)


def _w_in_spec(l, group):
    width, block = W_IN_GROUPS[group]
    return pl.BlockSpec((None, D_MODEL, width), lambda *_: (l, 0, block))


def _params(sem):
    return pltpu.CompilerParams(dimension_semantics=sem, vmem_limit_bytes=VMEM_LIMIT)


def _rms_scale(xf, n):
    return lax.rsqrt(jnp.sum(xf * xf, axis=-1, keepdims=True) * (1.0 / n) + EPS)


def _normed_bf16(xf, g_ref):
    return (xf * _rms_scale(xf, D_MODEL) * g_ref[...]).astype(BF16)


def _with_casts(body, n_in, n_out, n_cast):
    def kernel_fn(*refs):
        ins, srcs = refs[:n_in], refs[n_in:n_in + n_cast]
        outs = refs[n_in + n_cast:n_in + n_cast + n_out]
        dsts = refs[n_in + n_cast + n_out:n_in + 2 * n_cast + n_out]
        for src, dst in zip(srcs, dsts):
            dst[...] = src[...].astype(BF16)
        body(*ins, *outs, *refs[n_in + 2 * n_cast + n_out:])
    return kernel_fn


def _cast_specs(arrays, steps):
    ins, outs, shapes = [], [], []
    for a in arrays:
        blk = a.shape[:-2] + (a.shape[-2] // steps, a.shape[-1])
        idx = lambda i, nd=a.ndim: (0,) * (nd - 2) + (i, 0)
        ins.append(pl.BlockSpec(blk, idx))
        outs.append(pl.BlockSpec(blk, idx))
        shapes.append(jax.ShapeDtypeStruct(a.shape, BF16))
    return ins, outs, shapes


def _dot(a, b):
    return jnp.dot(a, b, preferred_element_type=F32)


def _rdot(a, b):
    n = a.shape[0] // DOT_ROWS
    return jnp.concatenate([_dot(a[r * DOT_ROWS:(r + 1) * DOT_ROWS], b) for r in range(n)], axis=0)


def _prologue_kernel(x_ref, g_ref, pos_ref, fm_ref, fr_ref, h_out, o_ref):
    h_out[...] = _normed_bf16(x_ref[...], g_ref)
    lane = lax.broadcasted_iota(jnp.int32, (pos_ref.shape[0], LANES), 1)
    pos = pos_ref[...]
    for n, (f_ref, is_x1, is_x2) in enumerate((
            (fm_ref, lane < MLA_ROPE // 2, (lane >= MLA_ROPE // 2) & (lane < MLA_ROPE)),
            (fr_ref, (lane % RET_DK) < RET_DK // 2, (lane % RET_DK) >= RET_DK // 2))):
        ang = pos * f_ref[...]
        sin_t = jnp.sin(ang)
        o_ref[:, (3 * n) * LANES:(3 * n + 1) * LANES] = jnp.cos(ang)
        o_ref[:, (3 * n + 1) * LANES:(3 * n + 2) * LANES] = jnp.where(is_x1, -sin_t, 0.0)
        o_ref[:, (3 * n + 2) * LANES:(3 * n + 3) * LANES] = jnp.where(is_x2, sin_t, 0.0)


def _prologue(x, g_mix, posf, mla_freq, ret_freq):
    tm = ROW_TILE
    row = pl.BlockSpec((tm, D_MODEL), lambda i: (i, 0))
    return pl.pallas_call(
        _prologue_kernel,
        grid=(SEQ // tm,),
        in_specs=[row, _lay(g_mix, 0), pl.BlockSpec((tm, 1), lambda i: (i, 0)), _const(mla_freq),
                  _const(ret_freq)],
        out_specs=[row, pl.BlockSpec((tm, 6 * LANES), lambda i: (i, 0))],
        out_shape=[jax.ShapeDtypeStruct((SEQ, D_MODEL), BF16), jax.ShapeDtypeStruct((SEQ, 6 * LANES), F32)],
        compiler_params=_params(("parallel",)),
        name="prologue",
    )(x, g_mix, posf, mla_freq, ret_freq)


def _rope_spec(which):
    return pl.BlockSpec((ROW_TILE, 3 * LANES), lambda i: (i, which))


def _rope_slab(x, cos_t, sin_a, sin_b):
    return (x * cos_t + pltpu.roll(x, LANES - 32, axis=1) * sin_a
            + pltpu.roll(x, 32, axis=1) * sin_b)


def _mla_pre_kernel(h_ref, rope_ref, wa_ref, gqa_ref, gkva_ref, wq_ref, wkv_ref,
                    gq_ref, gkn_ref, gkr_ref, *rest):
    if len(rest) == 3:
        qt_out, k_out, vt_out = rest
    else:
        wup_ref, wdn_ref, qt_out, k_out, vt_out, wg_out, wu_out, wd_out = rest
        wg_out[...] = wup_ref[:, :D_FF].astype(BF16)
        wu_out[...] = wup_ref[:, D_FF:].astype(BF16)
        wd_out[...] = wdn_ref[...].astype(BF16)
    z = _rdot(h_ref[...], wa_ref[...])
    zq = z[:, :Q_LORA]
    zkv = z[:, Q_LORA:Q_LORA + KV_LORA]
    lane = lax.broadcasted_iota(jnp.int32, (z.shape[0], LANES), 1)
    zkr = jnp.where(lane < MLA_ROPE, z[:, Q_LORA + KV_LORA:], 0.0)
    c_q = (zq * _rms_scale(zq, Q_LORA) * gqa_ref[...]).astype(BF16)
    c_kv = (zkv * _rms_scale(zkv, KV_LORA) * gkva_ref[...]).astype(BF16)

    cos_t, sin_a, sin_b = (rope_ref[:, n * LANES:(n + 1) * LANES] for n in range(3))

    q = _rdot(c_q, wq_ref[...])
    kv = _rdot(c_kv, wkv_ref[...])

    kr = zkr * gkr_ref[...]
    ss_kr = jnp.sum(zkr * zkr, axis=-1, keepdims=True)
    kr_rot = _rope_slab(kr, cos_t, sin_a, sin_b)
    ones = jnp.ones((V_ONES, z.shape[0]), BF16)
    for hd in range(MLA_HEADS):
        qb = q[:, hd * HEAD_PAD:(hd + 1) * HEAD_PAD]
        qn = qb * _rms_scale(qb, MLA_QK) * gq_ref[...]
        qt_out[hd, 0:LANES, :] = qn[:, :LANES].T.astype(BF16)
        qt_out[hd, LANES:, :] = _rope_slab(qn[:, LANES:], cos_t, sin_a, sin_b).T.astype(BF16)
        vh = kv[:, MLA_HEADS * MLA_NOPE + hd * MLA_V:MLA_HEADS * MLA_NOPE + (hd + 1) * MLA_V]
        vt_out[hd, 0, 0:MLA_V, :] = vh.T.astype(BF16)
        vt_out[hd, 0, MLA_V:, :] = ones
        kn = kv[:, hd * MLA_NOPE:(hd + 1) * MLA_NOPE]
        r_k = lax.rsqrt((jnp.sum(kn * kn, axis=-1, keepdims=True) + ss_kr) * (1.0 / MLA_QK) + EPS)
        k_out[:, hd * HEAD_PAD:hd * HEAD_PAD + LANES] = (kn * r_k * gkn_ref[...]).astype(BF16)
        k_out[:, hd * HEAD_PAD + LANES:(hd + 1) * HEAD_PAD] = (kr_rot * r_k).astype(BF16)


def _mla_pre(l, h, rope, wa, g_qa, g_kva, wq, wkv, gq, gkn, gkr, w_up_f32=None, w_down_f32=None):
    tm = ROW_TILE
    n = SEQ // tm
    row = lambda w: pl.BlockSpec((tm, w), lambda i: (i, 0))
    in_specs = ([row(D_MODEL), _rope_spec(0)]
                + [_lay(a, l) for a in (wa, g_qa, g_kva, wq, wkv, gq, gkn, gkr)])
    out_specs = [pl.BlockSpec((MLA_HEADS, HEAD_PAD, tm), lambda i: (0, 0, i)),
                 row(MLA_HEADS * HEAD_PAD),
                 pl.BlockSpec((MLA_HEADS, 1, MLA_V + V_ONES, tm), lambda i: (0, i, 0, 0))]
    out_shape = [jax.ShapeDtypeStruct((MLA_HEADS, HEAD_PAD, SEQ), BF16),
                 jax.ShapeDtypeStruct((SEQ, MLA_HEADS * HEAD_PAD), BF16),
                 jax.ShapeDtypeStruct((MLA_HEADS, SEQ // tm, MLA_V + V_ONES, tm), BF16)]
    args = [h, rope, wa, g_qa, g_kva, wq, wkv, gq, gkn, gkr]
    if w_up_f32 is not None:
        ur, dr = D_MODEL // n, D_FF // n
        in_specs += [pl.BlockSpec((None, ur, 2 * D_FF), lambda i: (l, i, 0)),
                     pl.BlockSpec((None, dr, D_MODEL), lambda i: (l, i, 0))]
        out_specs += [pl.BlockSpec((ur, D_FF), lambda i: (i, 0)), pl.BlockSpec((ur, D_FF), lambda i: (i, 0)),
                      pl.BlockSpec((dr, D_MODEL), lambda i: (i, 0))]
        out_shape += [jax.ShapeDtypeStruct((D_MODEL, D_FF), BF16), jax.ShapeDtypeStruct((D_MODEL, D_FF), BF16),
                      jax.ShapeDtypeStruct((D_FF, D_MODEL), BF16)]
        args += [w_up_f32, w_down_f32]
    return pl.pallas_call(
        _mla_pre_kernel,
        grid=(n,),
        in_specs=in_specs,
        out_specs=out_specs,
        out_shape=out_shape,
        compiler_params=_params(("parallel",)),
        name="mla_pre",
    )(*args)


def _flash_kernel(qt_ref, k_ref, vt_ref, o_ref, m_scr, acc_scr, st_a, st_b):
    t = ATT_TILE
    i = pl.program_id(1)
    sub = SUBLANES
    pack = 2 * SUBLANES
    vrows = MLA_V + V_ONES
    m_scr[...] = jnp.full(m_scr.shape, MASK_VALUE, F32)
    acc_scr[...] = jnp.zeros(acc_scr.shape, F32)

    def logits(j, dst):
        start = pl.multiple_of(j * t, t)
        for hh in range(ATT_HEADS_PER_STEP):
            dst[hh] = _rdot(k_ref[pl.ds(start, t), hh * HEAD_PAD:(hh + 1) * HEAD_PAD], qt_ref[hh])

    def softmax_pv(src, j, masked):
        if masked:
            key = lax.broadcasted_iota(jnp.int32, (t, t), 0)
            qry = lax.broadcasted_iota(jnp.int32, (t, t), 1)
            keep = key <= qry
        for hh in range(ATT_HEADS_PER_STEP):
            st = src[hh]
            if masked:
                st = jnp.where(keep, st, MASK_VALUE)
            s3 = st.reshape(t // pack, pack, t)
            mx = jnp.max(s3, axis=0)
            mx = jnp.maximum(mx[:sub], mx[sub:])
            for shift in (4, 2, 1):
                mx = jnp.maximum(mx, pltpu.roll(mx, shift, axis=0))
            m_prev = m_scr[hh]
            m_new = jnp.maximum(m_prev, mx)
            alpha = jnp.exp2(m_prev - m_new)
            m2 = jnp.concatenate([m_new, m_new], axis=0)
            p = jnp.exp2((s3 - m2[None]).astype(BF16)).reshape(t, t)
            pv = _dot(vt_ref[hh, j], p)
            acc = acc_scr[hh].reshape(vrows // sub, sub, t) * alpha[None] + pv.reshape(vrows // sub, sub, t)
            acc_scr[hh] = acc.reshape(vrows, t)
            m_scr[hh] = m_new

    logits(0, st_a)

    def pair(m, carry):
        logits(2 * m + 1, st_b)
        softmax_pv(st_a, 2 * m, False)
        logits(2 * m + 2, st_a)
        softmax_pv(st_b, 2 * m + 1, False)
        return carry

    lax.fori_loop(0, lax.shift_right_logical(i, 1), pair, 0)
    odd = (i & 1) == 1

    @pl.when(odd)
    def _():
        logits(i, st_b)
        softmax_pv(st_a, i - 1, False)
        softmax_pv(st_b, i, True)

    @pl.when(jnp.logical_not(odd))
    def _():
        softmax_pv(st_a, i, True)

    for hh in range(ATT_HEADS_PER_STEP):
        num = acc_scr[hh, 0:MLA_V, :].reshape(MLA_V // sub, sub, t)
        den = acc_scr[hh, MLA_V:MLA_V + sub, :]
        out_t = (num / den[None]).reshape(MLA_V, t)
        o_ref[:, hh * MLA_V:(hh + 1) * MLA_V] = out_t.T.astype(o_ref.dtype)


def _flash(qt, k, vt):
    t, hp = ATT_TILE, ATT_HEADS_PER_STEP
    vrows = MLA_V + V_ONES
    return pl.pallas_call(
        _flash_kernel,
        grid=(MLA_HEADS // hp, SEQ // t),
        in_specs=[pl.BlockSpec((hp, HEAD_PAD, t), lambda h, i: (h, 0, i)),
                  pl.BlockSpec((SEQ, hp * HEAD_PAD), lambda h, i: (0, h), pipeline_mode=pl.Buffered(1)),
                  pl.BlockSpec((hp, SEQ // t, vrows, t), lambda h, i: (h, 0, 0, 0),
                               pipeline_mode=pl.Buffered(1))],
        out_specs=pl.BlockSpec((t, hp * MLA_V), lambda h, i: (i, h)),
        out_shape=jax.ShapeDtypeStruct((SEQ, MLA_HEADS * MLA_V), BF16),
        scratch_shapes=[pltpu.VMEM((hp, SUBLANES, t), F32), pltpu.VMEM((hp, vrows, t), F32),
                        pltpu.VMEM((hp, t, t), F32), pltpu.VMEM((hp, t, t), F32)],
        compiler_params=_params(("parallel", "arbitrary")),
        name="flash",
    )(qt, k, vt)


def _sconv_kernel(h_ref, w_ref, wsc_ref, o_ref, buf):
    tm = ROW_TILE
    z = _rdot(h_ref[...], w_ref[...])

    @pl.when(pl.program_id(0) == 0)
    def _():
        buf[0:SUBLANES, :] = jnp.zeros((SUBLANES, GROUP), F32)

    g = z[:, GROUP:2 * GROUP] * z[:, 2 * GROUP:]
    buf[SUBLANES:, :] = g
    w = wsc_ref[...]
    conv = (w[2:3, :] * g + w[1:2, :] * buf[SUBLANES - 1:SUBLANES - 1 + tm, :]
            + w[0:1, :] * buf[SUBLANES - 2:SUBLANES - 2 + tm, :])
    o_ref[...] = (z[:, :GROUP] * conv).astype(o_ref.dtype)
    buf[0:SUBLANES, :] = buf[tm:tm + SUBLANES, :]


def _sconv(l, h, w_in, wsc, cast=()):
    tm = ROW_TILE
    n = SEQ // tm
    c_in, c_out, c_shape = _cast_specs(cast, n)
    return pl.pallas_call(
        _with_casts(_sconv_kernel, 3, 1, len(cast)),
        grid=(n,),
        in_specs=[pl.BlockSpec((tm, D_MODEL), lambda i: (i, 0)),
                  _w_in_spec(l, "sconv"), _lay(wsc, l)] + c_in,
        out_specs=[pl.BlockSpec((tm, GROUP), lambda i: (i, 0))] + c_out,
        out_shape=[jax.ShapeDtypeStruct((SEQ, GROUP), BF16)] + c_shape,
        scratch_shapes=[pltpu.VMEM((tm + SUBLANES, GROUP), F32)],
        compiler_params=_params(("arbitrary",)),
        name="sconv",
    )(h, w_in, wsc, *cast)


def _conformer_kernel(h_ref, w_ref, wcf_ref, bcf_ref, gln_ref, bln_ref, o_ref, buf, wbuf, uscr):
    tm = ROW_TILE
    nslab = GROUP // LANES
    z = _rdot(h_ref[...], w_ref[...])

    @pl.when(pl.program_id(0) == 0)
    def _():
        buf[:, 0:CF_HALO, :] = jnp.zeros((nslab, CF_HALO, LANES), F32)

    glu = z[:, :GROUP] * jax.nn.sigmoid(z[:, GROUP:])
    for sl in range(nslab):
        buf[sl, CF_HALO:, :] = glu[:, sl * LANES:(sl + 1) * LANES]
    base = CF_HALO - (CF_K - 1)
    for phase in range(1, SUBLANES):
        wbuf[phase - 1] = buf[:, phase:phase + tm + CF_HALO - SUBLANES, :]

    def tap_rows(o, r0, sl):
        phase = o % SUBLANES
        start = pl.multiple_of(r0 + (o - phase), SUBLANES)
        if phase == 0:
            return buf[sl, pl.ds(start, CF_ROWS), :]
        return wbuf[phase - 1, sl, pl.ds(start, CF_ROWS), :]

    def row_block(rb, carry):
        r0 = pl.multiple_of(rb * CF_ROWS, CF_ROWS)
        for sl in range(nslab):
            lanes = slice(sl * LANES, (sl + 1) * LANES)
            acc = jnp.broadcast_to(bcf_ref[:, lanes], (CF_ROWS, LANES))
            for j in range(CF_K):
                acc = acc + wcf_ref[j:j + 1, lanes] * tap_rows(base + j, r0, sl)
            uscr[pl.ds(r0, CF_ROWS), lanes] = acc
        return carry

    lax.fori_loop(0, tm // CF_ROWS, row_block, 0)
    u = uscr[...]
    mu = jnp.mean(u, axis=-1, keepdims=True)
    uc = u - mu
    y = uc * lax.rsqrt(jnp.mean(uc * uc, axis=-1, keepdims=True) + EPS)
    y = y * gln_ref[...] + bln_ref[...]
    o_ref[...] = (y * jax.nn.sigmoid(y)).astype(o_ref.dtype)
    buf[:, 0:CF_HALO, :] = buf[:, tm:tm + CF_HALO, :]


def _conformer(l, h, w_in, wcf, bcf, gln, bln, cast=()):
    tm = ROW_TILE
    n = SEQ // tm
    c_in, c_out, c_shape = _cast_specs(cast, n)
    return pl.pallas_call(
        _with_casts(_conformer_kernel, 6, 1, len(cast)),
        grid=(n,),
        in_specs=[pl.BlockSpec((tm, D_MODEL), lambda i: (i, 0)), _w_in_spec(l, "conformer")]
        + [_lay(a, l) for a in (wcf, bcf, gln, bln)] + c_in,
        out_specs=[pl.BlockSpec((tm, GROUP), lambda i: (i, 0))] + c_out,
        out_shape=[jax.ShapeDtypeStruct((SEQ, GROUP), BF16)] + c_shape,
        scratch_shapes=[pltpu.VMEM((GROUP // LANES, tm + CF_HALO, LANES), F32),
                        pltpu.VMEM((SUBLANES - 1, GROUP // LANES, tm + CF_HALO - SUBLANES, LANES), F32),
                        pltpu.VMEM((tm, GROUP), F32)],
        compiler_params=_params(("arbitrary",)),
        name="conformer",
    )(h, w_in, wcf, bcf, gln, bln, *cast)


def _retention_kernel(h_ref, rope_ref, wqk_ref, wv_ref, wg_ref, qmask_ref, kdec_ref, qdec_ref,
                      inner_ref, cdec_ref, gret_ref, o_ref, state):
    tm = ROW_TILE
    c = RET_CHUNK
    dk2 = RET_HEADS * RET_DK
    h = h_ref[...]
    zqk = _rdot(h, wqk_ref[...])
    v = _rdot(h, wv_ref[...])
    gate = _rdot(h, wg_ref[...])

    @pl.when(pl.program_id(0) == 0)
    def _():
        state[...] = jnp.zeros(state.shape, F32)

    half = RET_DK // 2
    cos_t, sin_a, sin_b = (rope_ref[:, n * LANES:(n + 1) * LANES] for n in range(3))

    def rot(t):
        return jnp.concatenate(
            [x * cos_t + pltpu.roll(x, LANES - half, axis=1) * sin_a + pltpu.roll(x, half, axis=1) * sin_b
             for x in (t[:, :LANES], t[:, LANES:])], axis=1)

    q = rot(zqk[:, :dk2])
    k = rot(zqk[:, dk2:]) * (RET_DK ** -0.5)

    for ci in range(tm // c):
        rows = slice(ci * c, (ci + 1) * c)
        qc = q[rows]
        kt = k[rows].T
        kt_b = kt.astype(BF16)
        for hd in range(RET_HEADS):
            cols = slice(hd * RET_DV, (hd + 1) * RET_DV)
            qm = (qc * qmask_ref[hd]).astype(BF16)
            vh = v[rows, cols].astype(BF16)
            sc = _dot(qm, kt_b) * inner_ref[hd]
            st = state[hd]
            o = _dot(sc.astype(BF16), vh) + _dot(qm, st.astype(BF16)) * qdec_ref[hd]
            state[hd] = st * cdec_ref[hd] + _dot((kt * kdec_ref[hd]).astype(BF16), vh)
            on = o * _rms_scale(o, RET_DV) * gret_ref[hd]
            gh = gate[rows, cols]
            o_ref[rows, cols] = (gh * jax.nn.sigmoid(gh) * on).astype(o_ref.dtype)


def _retention(l, h, rope, w_in, qmask, kdec, qdec, inner, cdec, gret):
    tm = ROW_TILE
    return pl.pallas_call(
        _retention_kernel,
        grid=(SEQ // tm,),
        in_specs=[pl.BlockSpec((tm, D_MODEL), lambda i: (i, 0)), _rope_spec(1), _w_in_spec(l, "ret_qk"),
                  _w_in_spec(l, "ret_v"), _w_in_spec(l, "ret_gate")]
        + [_const(a) for a in (qmask, kdec, qdec, inner, cdec)] + [_lay(gret, l)],
        out_specs=pl.BlockSpec((tm, GROUP), lambda i: (i, 0)),
        out_shape=jax.ShapeDtypeStruct((SEQ, GROUP), BF16),
        scratch_shapes=[pltpu.VMEM((RET_HEADS, RET_HEADS * RET_DK, RET_DV), F32)],
        compiler_params=_params(("arbitrary",)),
        name="retention",
    )(h, rope, w_in, w_in, w_in, qmask, kdec, qdec, inner, cdec, gret)


def _out_proj_kernel(x_ref, ya_ref, yb_ref, yc_ref, yd_ref, wo_ref, g_ref, o_ref, h_out):
    acc = x_ref[...]
    for n, y_ref in enumerate((ya_ref, yb_ref, yc_ref, yd_ref)):
        acc = acc + _dot(y_ref[...], wo_ref[n * GROUP:(n + 1) * GROUP, :])
    o_ref[...] = acc
    h_out[...] = _normed_bf16(acc, g_ref)


def _out_proj(l, x, ya, yb, yc, yd, wo, g_ffn):
    tm = ROW_TILE
    row = lambda n: pl.BlockSpec((tm, n), lambda i: (i, 0))
    return pl.pallas_call(
        _out_proj_kernel,
        grid=(SEQ // tm,),
        in_specs=[row(D_MODEL), row(GROUP), row(GROUP), row(GROUP), row(GROUP), _lay(wo, l),
                  _lay(g_ffn, l)],
        out_specs=[row(D_MODEL), row(D_MODEL)],
        out_shape=[jax.ShapeDtypeStruct((SEQ, D_MODEL), F32), jax.ShapeDtypeStruct((SEQ, D_MODEL), BF16)],
        compiler_params=_params(("parallel",)),
        name="out_proj",
    )(x, ya, yb, yc, yd, wo, g_ffn)


def _ffn_kernel(x_ref, h_ref, wg_ref, wu_ref, cg_ref, cu_ref, wd_ref, *rest):
    tm = ROW_TILE
    i = pl.program_id(0)
    j = pl.program_id(1)
    if len(rest) == 3:
        o_ref, ubuf, carry = rest
    else:
        nxt_in, (o_ref, *nxt_out), (ubuf, carry) = rest[:3], rest[3:7], rest[7:]
        for src, dst in zip(nxt_in, nxt_out):
            dst[...] = src[...].astype(BF16)

    @pl.when(j == 0)
    def _():
        o_ref[...] = x_ref[...]

    @pl.when(i == 0)
    def _():
        carry[j] = jnp.zeros(carry.shape[1:], F32)

    ts = FF_TILE // FF_SUB
    h = h_ref[...]

    def conv(part, s, u, c_ref):
        cols = slice(s * ts, (s + 1) * ts)
        ubuf[part, s, 0:SUBLANES, :] = carry[j, part, :, cols]
        ubuf[part, s, SUBLANES:, :] = u
        carry[j, part, :, cols] = u[tm - SUBLANES:, :]
        w = c_ref[:, cols]
        return (w[2:3, :] * u + w[1:2, :] * ubuf[part, s, SUBLANES - 1:SUBLANES - 1 + tm, :]
                + w[0:1, :] * ubuf[part, s, SUBLANES - 2:SUBLANES - 2 + tm, :])

    us = [(_rdot(h, wg_ref[:, s * ts:(s + 1) * ts]), _rdot(h, wu_ref[:, s * ts:(s + 1) * ts]))
          for s in range(FF_SUB)]
    for s, (ug, uu) in enumerate(us):
        gate = conv(0, s, ug, cg_ref)
        up = conv(1, s, uu, cu_ref)
        act = (gate * jax.nn.sigmoid(gate) * up).astype(BF16)
        o_ref[...] += _dot(act, wd_ref[s * ts:(s + 1) * ts, :])


def _ffn(l, x, h, wg, wu, wd, w_conv, w_up_f32, w_down_f32):
    tm, tf = ROW_TILE, FF_TILE
    nf, ni = D_FF // tf, SEQ // tm
    in_specs = [pl.BlockSpec((tm, D_MODEL), lambda i, j: (i, 0)),
                pl.BlockSpec((tm, D_MODEL), lambda i, j: (i, 0)),
                pl.BlockSpec((D_MODEL, tf), lambda i, j: (0, j)),
                pl.BlockSpec((D_MODEL, tf), lambda i, j: (0, j)),
                pl.BlockSpec((None, FFN_K, tf), lambda i, j: (l, 0, j)),
                pl.BlockSpec((None, FFN_K, tf), lambda i, j: (l, 0, j + nf)),
                pl.BlockSpec((tf, D_MODEL), lambda i, j: (j, 0))]
    out_specs = [pl.BlockSpec((tm, D_MODEL), lambda i, j: (i, 0))]
    out_shape = [jax.ShapeDtypeStruct((SEQ, D_MODEL), F32)]
    args = [x, h, wg, wu, w_conv, w_conv, wd]
    last = l == DEPTH - 1
    if not last:
        ur, dc = D_MODEL // ni, D_MODEL // ni
        in_specs += [pl.BlockSpec((None, ur, tf), lambda i, j: (l + 1, i, j)),
                     pl.BlockSpec((None, ur, tf), lambda i, j: (l + 1, i, j + nf)),
                     pl.BlockSpec((None, tf, dc), lambda i, j: (l + 1, j, i))]
        out_specs += [pl.BlockSpec((ur, tf), lambda i, j: (i, j)),
                      pl.BlockSpec((ur, tf), lambda i, j: (i, j)),
                      pl.BlockSpec((tf, dc), lambda i, j: (j, i))]
        out_shape += [jax.ShapeDtypeStruct((D_MODEL, D_FF), BF16), jax.ShapeDtypeStruct((D_MODEL, D_FF), BF16),
                      jax.ShapeDtypeStruct((D_FF, D_MODEL), BF16)]
        args += [w_up_f32, w_up_f32, w_down_f32]
    outs = pl.pallas_call(
        _ffn_kernel,
        grid=(ni, nf),
        in_specs=in_specs,
        out_specs=out_specs,
        out_shape=out_shape,
        scratch_shapes=[pltpu.VMEM((2, FF_SUB, tm + SUBLANES, tf // FF_SUB), F32),
                        pltpu.VMEM((nf, 2, SUBLANES, tf), F32)],
        compiler_params=_params(("arbitrary", "arbitrary")),
        name="ffn",
    )(*args)
    return (outs[0], None, None, None) if last else tuple(outs)


def _ple_kernel(x_ref, p_ref, g_ref, wpg_ref, wpe_ref, *rest):
    o_ref = rest[-2] if len(rest) == 3 else rest[-1]
    hn = _normed_bf16(x_ref[...], g_ref)
    pb = p_ref[...].astype(BF16)
    for n in range(D_MODEL // GROUP):
        cols = slice(n * GROUP, (n + 1) * GROUP)
        gate = jax.nn.sigmoid(_dot(hn, wpg_ref[:, cols]))
        o_ref[:, cols] = x_ref[:, cols] + _dot(pb, wpe_ref[:, cols]) * gate
    if len(rest) == 3:
        gnext_ref, _, h_out = rest
        h_out[...] = _normed_bf16(o_ref[...], gnext_ref)


def _ple(l, x, p, g, wpg, wpe, g_mix):
    tm = ROW_TILE
    row = lambda n: pl.BlockSpec((tm, n), lambda i: (i, 0))
    last = l == DEPTH - 1
    in_specs = [row(D_MODEL), pl.BlockSpec((None, None, tm, PLE_DIM), lambda i: (l, 0, i, 0)),
                _lay(g, l), _lay(wpg, l), _lay(wpe, l)]
    out_specs = [row(D_MODEL)]
    out_shape = [jax.ShapeDtypeStruct((SEQ, D_MODEL), F32)]
    args = [x, p, g, wpg, wpe]
    if not last:
        in_specs.append(_lay(g_mix, l + 1))
        out_specs.append(row(D_MODEL))
        out_shape.append(jax.ShapeDtypeStruct((SEQ, D_MODEL), BF16))
        args.append(g_mix)
    outs = pl.pallas_call(
        _ple_kernel,
        grid=(SEQ // tm,),
        in_specs=in_specs,
        out_specs=out_specs,
        out_shape=out_shape,
        compiler_params=_params(("parallel",)),
        name="ple",
    )(*args)
    return (outs[0], None) if last else (outs[0], outs[1])


def _retention_tables():
    c = RET_CHUNK
    lg = jnp.log(1.0 - 2.0 ** (-5.0 - jnp.arange(RET_HEADS, dtype=F32)))
    idx = jnp.arange(c, dtype=F32)
    rel = idx[:, None] - idx[None, :]
    inner = jnp.where(rel[None] >= 0, jnp.exp(lg[:, None, None] * rel[None]), 0.0)
    qdec = jnp.exp(lg[:, None] * (idx + 1.0))[:, :, None]
    kdec_t = jnp.exp(lg[:, None] * (c - 1.0 - idx))
    cdec = jnp.exp(lg * c)
    lane = np.arange(RET_HEADS * RET_DK)
    head_of_lane = lane // RET_DK
    hmask = jnp.asarray(head_of_lane[None, :] == np.arange(RET_HEADS)[:, None], F32)
    qmask = hmask[:, None, :]
    kdec = hmask[:, :, None] * kdec_t[:, None, :]
    cdec_b = jnp.broadcast_to(cdec[:, None, None], (RET_HEADS, 1, RET_DV))
    return qmask, kdec, qdec, inner, cdec_b


def _lane_freq(inv):
    return jnp.tile(inv, LANES // inv.shape[0])[None, :]


def kernel(x, p, positions, g_mix, w_in, g_qa, g_kva, w_q_up, w_kv_up, g_qn, g_kn, w_sc, w_cf,
           b_cf, g_cf_ln, b_cf_ln, g_ret, w_o, g_ffn, w_up, w_ffn_conv, w_down, g_pe, w_pe, w_pg):
    assert x.shape == (1, SEQ, D_MODEL)
    xs = x[0]
    posf = positions[0].astype(F32)[:, None]
    rows = lambda a: a[:, None, :]

    mla_freq = _lane_freq(ROPE_BASE ** (-jnp.arange(0, MLA_ROPE, 2, dtype=F32) / MLA_ROPE))
    ret_freq = _lane_freq(1.0 / (10000.0 ** jnp.linspace(0.0, 1.0, RET_DK // 2, dtype=F32)))
    qmask, kdec, qdec, inner, cdec = _retention_tables()

    w_in_r = jnp.pad(w_in.astype(BF16), ((0, 0), (0, 0), (W_IN_SHIFT, 0)))
    wa = w_in[:, :, :MLA_IN].astype(BF16)
    wq = jnp.pad(w_q_up.reshape(DEPTH, Q_LORA, MLA_HEADS, MLA_QK),
                 ((0, 0), (0, 0), (0, 0), (0, HEAD_PAD - MLA_QK))
                 ).reshape(DEPTH, Q_LORA, MLA_HEADS * HEAD_PAD).astype(BF16)
    wkv4 = w_kv_up.reshape(DEPTH, KV_LORA, MLA_HEADS, MLA_NOPE + MLA_V)
    wkv = jnp.concatenate([wkv4[..., :MLA_NOPE].reshape(DEPTH, KV_LORA, -1),
                           wkv4[..., MLA_NOPE:].reshape(DEPTH, KV_LORA, -1)], axis=2).astype(BF16)
    gq = jnp.pad(g_qn * (MLA_QK ** -0.5 * LOG2_E), ((0, 0), (0, HEAD_PAD - MLA_QK)))
    gkr = jnp.pad(g_kn[:, MLA_NOPE:], ((0, 0), (0, LANES - MLA_ROPE)))
    gm, g_ret4 = rows(g_mix), g_ret[:, :, None, :]

    hm, rope = _prologue(xs, gm, posf, mla_freq, ret_freq)
    for l in range(DEPTH):
        mla = _mla_pre(l, hm, rope, wa, rows(g_qa), rows(g_kva), wq, wkv, rows(gq),
                       rows(g_kn[:, :MLA_NOPE]), rows(gkr), *((w_up, w_down) if l == 0 else ()))
        qt, k, vt = mla[:3]
        if l == 0:
            wg, wu, wd = mla[3:]
        y_a = _flash(qt, k, vt)
        first = l == 0
        y_b, *cast_b = _sconv(l, hm, w_in_r, w_sc, cast=(w_o,) if first else ())
        y_c, *cast_c = _conformer(l, hm, w_in_r, w_cf, rows(b_cf), rows(g_cf_ln), rows(b_cf_ln),
                                  cast=(w_pg, w_pe) if first else ())
        if first:
            (w_o_b,), (w_pg_b, w_pe_b) = cast_b, cast_c
        y_d = _retention(l, hm, rope, w_in_r, qmask, kdec, qdec, inner, cdec, g_ret4)
        xs, hf = _out_proj(l, xs, y_a, y_b, y_c, y_d, w_o_b, rows(g_ffn))
        xs, wg, wu, wd = _ffn(l, xs, hf, wg, wu, wd, w_ffn_conv, w_up, w_down)
        xs, hm = _ple(l, xs, p, rows(g_pe), w_pg_b, w_pe_b, gm)
    return xs[None]
```

```python
import numpy as np
import jax
import jax.numpy as jnp
from jax import lax
from jax.experimental import pallas as pl
from jax.experimental.pallas import tpu as pltpu

F32 = jnp.float32
BF16 = jnp.bfloat16

D_MODEL = 2048
SEQ = 8192
DEPTH = 2
GROUP = 512
MLA_HEADS = 4
MLA_NOPE = 128
MLA_ROPE = 64
MLA_V = 128
MLA_QK = MLA_NOPE + MLA_ROPE
Q_LORA = 512
KV_LORA = 256
ROPE_BASE = 10000.0
SC_K = 3
CF_K = 31
RET_HEADS = 4
RET_DV = 128
RET_DK = 64
RET_CHUNK = 128
D_FF = 5632
FFN_K = 3
PLE_DIM = 256
EPS = 1e-6
N_IN = Q_LORA + KV_LORA + MLA_ROPE + 3 * GROUP + 2 * GROUP + 2 * RET_HEADS * (RET_DK + RET_DV)

LANES = 128
SUBLANES = 8
HEAD_PAD = 2 * LANES
ROW_TILE = 512
FF_TILE = 512
FF_SUB = 2
DOT_ROWS = 128
ATT_TILE = 512
ATT_HEADS_PER_STEP = 4
V_ONES = 16
CF_HALO = 32
CF_ROWS = 128
MLA_IN = Q_LORA + KV_LORA + LANES
W_IN_SHIFT = 3 * GROUP - (Q_LORA + KV_LORA + MLA_ROPE)
W_IN_GROUPS = {"sconv": (3 * GROUP, 1), "conformer": (2 * GROUP, 3), "ret_qk": (GROUP, 8),
               "ret_v": (GROUP, 9), "ret_gate": (GROUP, 10)}
VMEM_LIMIT = 56 * 1024 * 1024
MASK_VALUE = -1e30
LOG2_E = 1.4426950408889634


def _lay(a, l):
    return pl.BlockSpec((None,) + a.shape[1:], lambda *_: (l,) + (0,) * (a.ndim - 1))


def _const(a):
    return pl.BlockSpec(a.shape, lambda *_: (0,) * a.ndim)


def _w_in_spec(l, group):
    width, block = W_IN_GROUPS[group]
    return pl.BlockSpec((None, D_MODEL, width), lambda *_: (l, 0, block))


def _params(sem):
    return pltpu.CompilerParams(dimension_semantics=sem, vmem_limit_bytes=VMEM_LIMIT)


def _rms_scale(xf, n):
    return lax.rsqrt(jnp.sum(xf * xf, axis=-1, keepdims=True) * (1.0 / n) + EPS)


def _normed_bf16(xf, g_ref):
    return (xf * _rms_scale(xf, D_MODEL) * g_ref[...]).astype(BF16)


def _with_casts(body, n_in, n_out, n_cast):
    def kernel_fn(*refs):
        ins, srcs = refs[:n_in], refs[n_in:n_in + n_cast]
        outs = refs[n_in + n_cast:n_in + n_cast + n_out]
        dsts = refs[n_in + n_cast + n_out:n_in + 2 * n_cast + n_out]
        for src, dst in zip(srcs, dsts):
            dst[...] = src[...].astype(BF16)
        body(*ins, *outs, *refs[n_in + 2 * n_cast + n_out:])
    return kernel_fn


def _cast_specs(arrays, steps):
    ins, outs, shapes = [], [], []
    for a in arrays:
        blk = a.shape[:-2] + (a.shape[-2] // steps, a.shape[-1])
        idx = lambda i, nd=a.ndim: (0,) * (nd - 2) + (i, 0)
        ins.append(pl.BlockSpec(blk, idx))
        outs.append(pl.BlockSpec(blk, idx))
        shapes.append(jax.ShapeDtypeStruct(a.shape, BF16))
    return ins, outs, shapes


def _dot(a, b):
    return jnp.dot(a, b, preferred_element_type=F32)


def _rdot(a, b):
    n = a.shape[0] // DOT_ROWS
    return jnp.concatenate([_dot(a[r * DOT_ROWS:(r + 1) * DOT_ROWS], b) for r in range(n)], axis=0)


def _prologue_kernel(x_ref, g_ref, pos_ref, fm_ref, fr_ref, h_out, o_ref):
    h_out[...] = _normed_bf16(x_ref[...], g_ref)
    lane = lax.broadcasted_iota(jnp.int32, (pos_ref.shape[0], LANES), 1)
    pos = pos_ref[...]
    for n, (f_ref, is_x1, is_x2) in enumerate((
            (fm_ref, lane < MLA_ROPE // 2, (lane >= MLA_ROPE // 2) & (lane < MLA_ROPE)),
            (fr_ref, (lane % RET_DK) < RET_DK // 2, (lane % RET_DK) >= RET_DK // 2))):
        ang = pos * f_ref[...]
        sin_t = jnp.sin(ang)
        o_ref[:, (3 * n) * LANES:(3 * n + 1) * LANES] = jnp.cos(ang)
        o_ref[:, (3 * n + 1) * LANES:(3 * n + 2) * LANES] = jnp.where(is_x1, -sin_t, 0.0)
        o_ref[:, (3 * n + 2) * LANES:(3 * n + 3) * LANES] = jnp.where(is_x2, sin_t, 0.0)


def _prologue(x, g_mix, posf, mla_freq, ret_freq):
    tm = ROW_TILE
    row = pl.BlockSpec((tm, D_MODEL), lambda i: (i, 0))
    return pl.pallas_call(
        _prologue_kernel,
        grid=(SEQ // tm,),
        in_specs=[row, _lay(g_mix, 0), pl.BlockSpec((tm, 1), lambda i: (i, 0)), _const(mla_freq),
                  _const(ret_freq)],
        out_specs=[row, pl.BlockSpec((tm, 6 * LANES), lambda i: (i, 0))],
        out_shape=[jax.ShapeDtypeStruct((SEQ, D_MODEL), BF16), jax.ShapeDtypeStruct((SEQ, 6 * LANES), F32)],
        compiler_params=_params(("parallel",)),
        name="prologue",
    )(x, g_mix, posf, mla_freq, ret_freq)


def _rope_spec(which):
    return pl.BlockSpec((ROW_TILE, 3 * LANES), lambda i: (i, which))


def _rope_slab(x, cos_t, sin_a, sin_b):
    return (x * cos_t + pltpu.roll(x, LANES - 32, axis=1) * sin_a
            + pltpu.roll(x, 32, axis=1) * sin_b)


def _mla_pre_kernel(h_ref, rope_ref, wa_ref, gqa_ref, gkva_ref, wq_ref, wkv_ref,
                    gq_ref, gkn_ref, gkr_ref, *rest):
    if len(rest) == 3:
        qt_out, k_out, vt_out = rest
    else:
        wup_ref, wdn_ref, qt_out, k_out, vt_out, wg_out, wu_out, wd_out = rest
        wg_out[...] = wup_ref[:, :D_FF].astype(BF16)
        wu_out[...] = wup_ref[:, D_FF:].astype(BF16)
        wd_out[...] = wdn_ref[...].astype(BF16)
    nr = DOT_ROWS
    lane = lax.broadcasted_iota(jnp.int32, (nr, LANES), 1)
    ones = jnp.ones((V_ONES, nr), BF16)
    for r0 in range(0, h_ref.shape[0], nr):
        rows = slice(r0, r0 + nr)
        z = _dot(h_ref[rows, :], wa_ref[...])
        zq = z[:, :Q_LORA]
        zkv = z[:, Q_LORA:Q_LORA + KV_LORA]
        zkr = jnp.where(lane < MLA_ROPE, z[:, Q_LORA + KV_LORA:], 0.0)
        c_q = (zq * _rms_scale(zq, Q_LORA) * gqa_ref[...]).astype(BF16)
        c_kv = (zkv * _rms_scale(zkv, KV_LORA) * gkva_ref[...]).astype(BF16)
        cos_t, sin_a, sin_b = (rope_ref[rows, n * LANES:(n + 1) * LANES] for n in range(3))

        q = _dot(c_q, wq_ref[...])
        kv = _dot(c_kv, wkv_ref[...])

        kr = zkr * gkr_ref[...]
        ss_kr = jnp.sum(zkr * zkr, axis=-1, keepdims=True)
        kr_rot = _rope_slab(kr, cos_t, sin_a, sin_b)
        for hd in range(MLA_HEADS):
            qb = q[:, hd * HEAD_PAD:(hd + 1) * HEAD_PAD]
            qn = qb * _rms_scale(qb, MLA_QK) * gq_ref[...]
            qt_out[hd, 0:LANES, rows] = qn[:, :LANES].T.astype(BF16)
            qt_out[hd, LANES:, rows] = _rope_slab(qn[:, LANES:], cos_t, sin_a, sin_b).T.astype(BF16)
            vh = kv[:, MLA_HEADS * MLA_NOPE + hd * MLA_V:MLA_HEADS * MLA_NOPE + (hd + 1) * MLA_V]
            vt_out[hd, 0, 0:MLA_V, rows] = vh.T.astype(BF16)
            vt_out[hd, 0, MLA_V:, rows] = ones
            kn = kv[:, hd * MLA_NOPE:(hd + 1) * MLA_NOPE]
            r_k = lax.rsqrt((jnp.sum(kn * kn, axis=-1, keepdims=True) + ss_kr) * (1.0 / MLA_QK) + EPS)
            k_out[rows, hd * HEAD_PAD:hd * HEAD_PAD + LANES] = (kn * r_k * gkn_ref[...]).astype(BF16)
            k_out[rows, hd * HEAD_PAD + LANES:(hd + 1) * HEAD_PAD] = (kr_rot * r_k).astype(BF16)


def _mla_pre(l, h, rope, wa, g_qa, g_kva, wq, wkv, gq, gkn, gkr, w_up_f32=None, w_down_f32=None):
    tm = ROW_TILE
    n = SEQ // tm
    row = lambda w: pl.BlockSpec((tm, w), lambda i: (i, 0))
    in_specs = ([row(D_MODEL), _rope_spec(0)]
                + [_lay(a, l) for a in (wa, g_qa, g_kva, wq, wkv, gq, gkn, gkr)])
    out_specs = [pl.BlockSpec((MLA_HEADS, HEAD_PAD, tm), lambda i: (0, 0, i)),
                 row(MLA_HEADS * HEAD_PAD),
                 pl.BlockSpec((MLA_HEADS, 1, MLA_V + V_ONES, tm), lambda i: (0, i, 0, 0))]
    out_shape = [jax.ShapeDtypeStruct((MLA_HEADS, HEAD_PAD, SEQ), BF16),
                 jax.ShapeDtypeStruct((SEQ, MLA_HEADS * HEAD_PAD), BF16),
                 jax.ShapeDtypeStruct((MLA_HEADS, SEQ // tm, MLA_V + V_ONES, tm), BF16)]
    args = [h, rope, wa, g_qa, g_kva, wq, wkv, gq, gkn, gkr]
    if w_up_f32 is not None:
        ur, dr = D_MODEL // n, D_FF // n
        in_specs += [pl.BlockSpec((None, ur, 2 * D_FF), lambda i: (l, i, 0)),
                     pl.BlockSpec((None, dr, D_MODEL), lambda i: (l, i, 0))]
        out_specs += [pl.BlockSpec((ur, D_FF), lambda i: (i, 0)), pl.BlockSpec((ur, D_FF), lambda i: (i, 0)),
                      pl.BlockSpec((dr, D_MODEL), lambda i: (i, 0))]
        out_shape += [jax.ShapeDtypeStruct((D_MODEL, D_FF), BF16), jax.ShapeDtypeStruct((D_MODEL, D_FF), BF16),
                      jax.ShapeDtypeStruct((D_FF, D_MODEL), BF16)]
        args += [w_up_f32, w_down_f32]
    return pl.pallas_call(
        _mla_pre_kernel,
        grid=(n,),
        in_specs=in_specs,
        out_specs=out_specs,
        out_shape=out_shape,
        compiler_params=_params(("parallel",)),
        name="mla_pre",
    )(*args)


def _flash_kernel(qt_ref, k_ref, vt_ref, o_ref, m_scr, acc_scr, st_a, st_b):
    t = ATT_TILE
    i = pl.program_id(1)
    sub = SUBLANES
    pack = 2 * SUBLANES
    vrows = MLA_V + V_ONES
    m_scr[...] = jnp.full(m_scr.shape, MASK_VALUE, F32)
    acc_scr[...] = jnp.zeros(acc_scr.shape, F32)

    def logits(j, dst):
        start = pl.multiple_of(j * t, t)
        for hh in range(ATT_HEADS_PER_STEP):
            dst[hh] = _rdot(k_ref[pl.ds(start, t), hh * HEAD_PAD:(hh + 1) * HEAD_PAD], qt_ref[hh])

    def softmax_pv(src, j, masked):
        if masked:
            key = lax.broadcasted_iota(jnp.int32, (t, t), 0)
            qry = lax.broadcasted_iota(jnp.int32, (t, t), 1)
            keep = key <= qry
        for hh in range(ATT_HEADS_PER_STEP):
            st = src[hh]
            if masked:
                st = jnp.where(keep, st, MASK_VALUE)
            s3 = st.reshape(t // pack, pack, t)
            mx = jnp.max(s3, axis=0)
            mx = jnp.maximum(mx[:sub], mx[sub:])
            for shift in (4, 2, 1):
                mx = jnp.maximum(mx, pltpu.roll(mx, shift, axis=0))
            m_prev = m_scr[hh]
            m_new = jnp.maximum(m_prev, mx)
            alpha = jnp.exp2(m_prev - m_new)
            m2 = jnp.concatenate([m_new, m_new], axis=0)
            p = jnp.exp2((s3 - m2[None]).astype(BF16)).reshape(t, t)
            pv = _dot(vt_ref[hh, j], p)
            acc = acc_scr[hh].reshape(vrows // sub, sub, t) * alpha[None] + pv.reshape(vrows // sub, sub, t)
            acc_scr[hh] = acc.reshape(vrows, t)
            m_scr[hh] = m_new

    logits(0, st_a)

    def pair(m, carry):
        logits(2 * m + 1, st_b)
        softmax_pv(st_a, 2 * m, False)
        logits(2 * m + 2, st_a)
        softmax_pv(st_b, 2 * m + 1, False)
        return carry

    lax.fori_loop(0, lax.shift_right_logical(i, 1), pair, 0)
    odd = (i & 1) == 1

    @pl.when(odd)
    def _():
        logits(i, st_b)
        softmax_pv(st_a, i - 1, False)
        softmax_pv(st_b, i, True)

    @pl.when(jnp.logical_not(odd))
    def _():
        softmax_pv(st_a, i, True)

    for hh in range(ATT_HEADS_PER_STEP):
        num = acc_scr[hh, 0:MLA_V, :].reshape(MLA_V // sub, sub, t)
        den = acc_scr[hh, MLA_V:MLA_V + sub, :]
        out_t = (num / den[None]).reshape(MLA_V, t)
        o_ref[:, hh * MLA_V:(hh + 1) * MLA_V] = out_t.T.astype(o_ref.dtype)


def _flash(qt, k, vt):
    t, hp = ATT_TILE, ATT_HEADS_PER_STEP
    vrows = MLA_V + V_ONES
    return pl.pallas_call(
        _flash_kernel,
        grid=(MLA_HEADS // hp, SEQ // t),
        in_specs=[pl.BlockSpec((hp, HEAD_PAD, t), lambda h, i: (h, 0, i)),
                  pl.BlockSpec((SEQ, hp * HEAD_PAD), lambda h, i: (0, h), pipeline_mode=pl.Buffered(1)),
                  pl.BlockSpec((hp, SEQ // t, vrows, t), lambda h, i: (h, 0, 0, 0),
                               pipeline_mode=pl.Buffered(1))],
        out_specs=pl.BlockSpec((t, hp * MLA_V), lambda h, i: (i, h)),
        out_shape=jax.ShapeDtypeStruct((SEQ, MLA_HEADS * MLA_V), BF16),
        scratch_shapes=[pltpu.VMEM((hp, SUBLANES, t), F32), pltpu.VMEM((hp, vrows, t), F32),
                        pltpu.VMEM((hp, t, t), F32), pltpu.VMEM((hp, t, t), F32)],
        compiler_params=_params(("parallel", "arbitrary")),
        name="flash",
    )(qt, k, vt)


def _sconv_kernel(h_ref, w_ref, wsc_ref, o_ref, buf):
    tm = ROW_TILE
    @pl.when(pl.program_id(0) == 0)
    def _():
        buf[0:SUBLANES, :] = jnp.zeros((SUBLANES, GROUP), F32)

    w = wsc_ref[...]
    nr = DOT_ROWS
    for r0 in range(0, tm, nr):
        z = _dot(h_ref[r0:r0 + nr, :], w_ref[...])
        g = z[:, GROUP:2 * GROUP] * z[:, 2 * GROUP:]
        b0 = SUBLANES + r0
        buf[b0:b0 + nr, :] = g
        conv = w[2:3, :] * g + w[1:2, :] * buf[b0 - 1:b0 - 1 + nr, :] + w[0:1, :] * buf[b0 - 2:b0 - 2 + nr, :]
        o_ref[r0:r0 + nr, :] = (z[:, :GROUP] * conv).astype(o_ref.dtype)
    buf[0:SUBLANES, :] = buf[tm:tm + SUBLANES, :]


def _sconv(l, h, w_in, wsc, cast=()):
    tm = ROW_TILE
    n = SEQ // tm
    c_in, c_out, c_shape = _cast_specs(cast, n)
    return pl.pallas_call(
        _with_casts(_sconv_kernel, 3, 1, len(cast)),
        grid=(n,),
        in_specs=[pl.BlockSpec((tm, D_MODEL), lambda i: (i, 0)),
                  _w_in_spec(l, "sconv"), _lay(wsc, l)] + c_in,
        out_specs=[pl.BlockSpec((tm, GROUP), lambda i: (i, 0))] + c_out,
        out_shape=[jax.ShapeDtypeStruct((SEQ, GROUP), BF16)] + c_shape,
        scratch_shapes=[pltpu.VMEM((tm + SUBLANES, GROUP), F32)],
        compiler_params=_params(("arbitrary",)),
        name="sconv",
    )(h, w_in, wsc, *cast)


def _conformer_kernel(h_ref, w_ref, wcf_ref, bcf_ref, gln_ref, bln_ref, o_ref, buf, wbuf, uscr):
    tm = ROW_TILE
    nslab = GROUP // LANES
    @pl.when(pl.program_id(0) == 0)
    def _():
        buf[:, 0:CF_HALO, :] = jnp.zeros((nslab, CF_HALO, LANES), F32)

    base = CF_HALO - (CF_K - 1)
    head = CF_HALO - SUBLANES
    for phase in range(1, SUBLANES):
        wbuf[phase - 1, :, 0:head, :] = buf[:, phase:phase + head, :]
    for r0 in range(0, tm, DOT_ROWS):
        z = _dot(h_ref[r0:r0 + DOT_ROWS, :], w_ref[...])
        glu = z[:, :GROUP] * jax.nn.sigmoid(z[:, GROUP:])
        for sl in range(nslab):
            buf[sl, r0 + CF_HALO:r0 + CF_HALO + DOT_ROWS, :] = glu[:, sl * LANES:(sl + 1) * LANES]
        for phase in range(1, SUBLANES):
            wbuf[phase - 1, :, r0 + head:r0 + head + DOT_ROWS, :] = (
                buf[:, r0 + head + phase:r0 + head + phase + DOT_ROWS, :])

    def tap_rows(o, r0, sl):
        phase = o % SUBLANES
        start = pl.multiple_of(r0 + (o - phase), SUBLANES)
        if phase == 0:
            return buf[sl, pl.ds(start, CF_ROWS), :]
        return wbuf[phase - 1, sl, pl.ds(start, CF_ROWS), :]

    def row_block(rb, carry):
        r0 = pl.multiple_of(rb * CF_ROWS, CF_ROWS)
        for sl in range(nslab):
            lanes = slice(sl * LANES, (sl + 1) * LANES)
            acc = jnp.broadcast_to(bcf_ref[:, lanes], (CF_ROWS, LANES))
            for j in range(CF_K):
                acc = acc + wcf_ref[j:j + 1, lanes] * tap_rows(base + j, r0, sl)
            uscr[pl.ds(r0, CF_ROWS), lanes] = acc
        return carry

    lax.fori_loop(0, tm // CF_ROWS, row_block, 0)
    u = uscr[...]
    mu = jnp.mean(u, axis=-1, keepdims=True)
    uc = u - mu
    y = uc * lax.rsqrt(jnp.mean(uc * uc, axis=-1, keepdims=True) + EPS)
    y = y * gln_ref[...] + bln_ref[...]
    o_ref[...] = (y * jax.nn.sigmoid(y)).astype(o_ref.dtype)
    buf[:, 0:CF_HALO, :] = buf[:, tm:tm + CF_HALO, :]


def _conformer(l, h, w_in, wcf, bcf, gln, bln, cast=()):
    tm = ROW_TILE
    n = SEQ // tm
    c_in, c_out, c_shape = _cast_specs(cast, n)
    return pl.pallas_call(
        _with_casts(_conformer_kernel, 6, 1, len(cast)),
        grid=(n,),
        in_specs=[pl.BlockSpec((tm, D_MODEL), lambda i: (i, 0)), _w_in_spec(l, "conformer")]
        + [_lay(a, l) for a in (wcf, bcf, gln, bln)] + c_in,
        out_specs=[pl.BlockSpec((tm, GROUP), lambda i: (i, 0))] + c_out,
        out_shape=[jax.ShapeDtypeStruct((SEQ, GROUP), BF16)] + c_shape,
        scratch_shapes=[pltpu.VMEM((GROUP // LANES, tm + CF_HALO, LANES), F32),
                        pltpu.VMEM((SUBLANES - 1, GROUP // LANES, tm + CF_HALO - SUBLANES, LANES), F32),
                        pltpu.VMEM((tm, GROUP), F32)],
        compiler_params=_params(("arbitrary",)),
        name="conformer",
    )(h, w_in, wcf, bcf, gln, bln, *cast)


def _retention_kernel(h_ref, rope_ref, wqk_ref, wv_ref, wg_ref, qmask_ref, kdec_ref, qdec_ref,
                      inner_ref, cdec_ref, gret_ref, o_ref, state):
    tm = ROW_TILE
    c = RET_CHUNK
    dk2 = RET_HEADS * RET_DK

    @pl.when(pl.program_id(0) == 0)
    def _():
        state[...] = jnp.zeros(state.shape, F32)

    half = RET_DK // 2

    for ci in range(tm // c):
        rows = slice(ci * c, (ci + 1) * c)
        hb = h_ref[rows, :]
        zqk = _dot(hb, wqk_ref[...])
        v = _dot(hb, wv_ref[...])
        gate = _dot(hb, wg_ref[...])
        cos_t, sin_a, sin_b = (rope_ref[rows, n * LANES:(n + 1) * LANES] for n in range(3))

        def rot(t):
            return jnp.concatenate(
                [x * cos_t + pltpu.roll(x, LANES - half, axis=1) * sin_a + pltpu.roll(x, half, axis=1) * sin_b
                 for x in (t[:, :LANES], t[:, LANES:])], axis=1)

        qc = rot(zqk[:, :dk2])
        kt = (rot(zqk[:, dk2:]) * (RET_DK ** -0.5)).T
        kt_b = kt.astype(BF16)
        for hd in range(RET_HEADS):
            cols = slice(hd * RET_DV, (hd + 1) * RET_DV)
            qm = (qc * qmask_ref[hd]).astype(BF16)
            vh = v[:, cols].astype(BF16)
            sc = _dot(qm, kt_b) * inner_ref[hd]
            st = state[hd]
            o = _dot(sc.astype(BF16), vh) + _dot(qm, st.astype(BF16)) * qdec_ref[hd]
            state[hd] = st * cdec_ref[hd] + _dot((kt * kdec_ref[hd]).astype(BF16), vh)
            on = o * _rms_scale(o, RET_DV) * gret_ref[hd]
            gh = gate[:, cols]
            o_ref[rows, cols] = (gh * jax.nn.sigmoid(gh) * on).astype(o_ref.dtype)


def _retention(l, h, rope, w_in, qmask, kdec, qdec, inner, cdec, gret):
    tm = ROW_TILE
    return pl.pallas_call(
        _retention_kernel,
        grid=(SEQ // tm,),
        in_specs=[pl.BlockSpec((tm, D_MODEL), lambda i: (i, 0)), _rope_spec(1), _w_in_spec(l, "ret_qk"),
                  _w_in_spec(l, "ret_v"), _w_in_spec(l, "ret_gate")]
        + [_const(a) for a in (qmask, kdec, qdec, inner, cdec)] + [_lay(gret, l)],
        out_specs=pl.BlockSpec((tm, GROUP), lambda i: (i, 0)),
        out_shape=jax.ShapeDtypeStruct((SEQ, GROUP), BF16),
        scratch_shapes=[pltpu.VMEM((RET_HEADS, RET_HEADS * RET_DK, RET_DV), F32)],
        compiler_params=_params(("arbitrary",)),
        name="retention",
    )(h, rope, w_in, w_in, w_in, qmask, kdec, qdec, inner, cdec, gret)


def _out_proj_kernel(x_ref, ya_ref, yb_ref, yc_ref, yd_ref, wo_ref, g_ref, o_ref, h_out):
    acc = x_ref[...]
    for n, y_ref in enumerate((ya_ref, yb_ref, yc_ref, yd_ref)):
        acc = acc + _dot(y_ref[...], wo_ref[n * GROUP:(n + 1) * GROUP, :])
    o_ref[...] = acc
    h_out[...] = _normed_bf16(acc, g_ref)


def _out_proj(l, x, ya, yb, yc, yd, wo, g_ffn):
    tm = ROW_TILE
    row = lambda n: pl.BlockSpec((tm, n), lambda i: (i, 0))
    return pl.pallas_call(
        _out_proj_kernel,
        grid=(SEQ // tm,),
        in_specs=[row(D_MODEL), row(GROUP), row(GROUP), row(GROUP), row(GROUP), _lay(wo, l),
                  _lay(g_ffn, l)],
        out_specs=[row(D_MODEL), row(D_MODEL)],
        out_shape=[jax.ShapeDtypeStruct((SEQ, D_MODEL), F32), jax.ShapeDtypeStruct((SEQ, D_MODEL), BF16)],
        compiler_params=_params(("parallel",)),
        name="out_proj",
    )(x, ya, yb, yc, yd, wo, g_ffn)


def _ffn_kernel(x_ref, h_ref, wg_ref, wu_ref, cg_ref, cu_ref, wd_ref, *rest):
    tm = ROW_TILE
    i = pl.program_id(0)
    j = pl.program_id(1)
    if len(rest) == 3:
        o_ref, ubuf, carry = rest
    else:
        nxt_in, (o_ref, *nxt_out), (ubuf, carry) = rest[:3], rest[3:7], rest[7:]
        for src, dst in zip(nxt_in, nxt_out):
            dst[...] = src[...].astype(BF16)

    @pl.when(j == 0)
    def _():
        o_ref[...] = x_ref[...]

    @pl.when(i == 0)
    def _():
        carry[j] = jnp.zeros(carry.shape[1:], F32)

    ts = FF_TILE // FF_SUB
    h = h_ref[...]

    def conv(part, s, u, c_ref):
        cols = slice(s * ts, (s + 1) * ts)
        ubuf[part, s, 0:SUBLANES, :] = carry[j, part, :, cols]
        ubuf[part, s, SUBLANES:, :] = u
        carry[j, part, :, cols] = u[tm - SUBLANES:, :]
        w = c_ref[:, cols]
        return (w[2:3, :] * u + w[1:2, :] * ubuf[part, s, SUBLANES - 1:SUBLANES - 1 + tm, :]
                + w[0:1, :] * ubuf[part, s, SUBLANES - 2:SUBLANES - 2 + tm, :])

    us = [(_rdot(h, wg_ref[:, s * ts:(s + 1) * ts]), _rdot(h, wu_ref[:, s * ts:(s + 1) * ts]))
          for s in range(FF_SUB)]
    for s, (ug, uu) in enumerate(us):
        gate = conv(0, s, ug, cg_ref)
        up = conv(1, s, uu, cu_ref)
        act = (gate * jax.nn.sigmoid(gate) * up).astype(BF16)
        o_ref[...] += _dot(act, wd_ref[s * ts:(s + 1) * ts, :])


def _ffn(l, x, h, wg, wu, wd, w_conv, w_up_f32, w_down_f32):
    tm, tf = ROW_TILE, FF_TILE
    nf, ni = D_FF // tf, SEQ // tm
    in_specs = [pl.BlockSpec((tm, D_MODEL), lambda i, j: (i, 0)),
                pl.BlockSpec((tm, D_MODEL), lambda i, j: (i, 0)),
                pl.BlockSpec((D_MODEL, tf), lambda i, j: (0, j)),
                pl.BlockSpec((D_MODEL, tf), lambda i, j: (0, j)),
                pl.BlockSpec((None, FFN_K, tf), lambda i, j: (l, 0, j)),
                pl.BlockSpec((None, FFN_K, tf), lambda i, j: (l, 0, j + nf)),
                pl.BlockSpec((tf, D_MODEL), lambda i, j: (j, 0))]
    out_specs = [pl.BlockSpec((tm, D_MODEL), lambda i, j: (i, 0))]
    out_shape = [jax.ShapeDtypeStruct((SEQ, D_MODEL), F32)]
    args = [x, h, wg, wu, w_conv, w_conv, wd]
    last = l == DEPTH - 1
    if not last:
        ur, dc = D_MODEL // ni, D_MODEL // ni
        in_specs += [pl.BlockSpec((None, ur, tf), lambda i, j: (l + 1, i, j)),
                     pl.BlockSpec((None, ur, tf), lambda i, j: (l + 1, i, j + nf)),
                     pl.BlockSpec((None, tf, dc), lambda i, j: (l + 1, j, i))]
        out_specs += [pl.BlockSpec((ur, tf), lambda i, j: (i, j)),
                      pl.BlockSpec((ur, tf), lambda i, j: (i, j)),
                      pl.BlockSpec((tf, dc), lambda i, j: (j, i))]
        out_shape += [jax.ShapeDtypeStruct((D_MODEL, D_FF), BF16), jax.ShapeDtypeStruct((D_MODEL, D_FF), BF16),
                      jax.ShapeDtypeStruct((D_FF, D_MODEL), BF16)]
        args += [w_up_f32, w_up_f32, w_down_f32]
    outs = pl.pallas_call(
        _ffn_kernel,
        grid=(ni, nf),
        in_specs=in_specs,
        out_specs=out_specs,
        out_shape=out_shape,
        scratch_shapes=[pltpu.VMEM((2, FF_SUB, tm + SUBLANES, tf // FF_SUB), F32),
                        pltpu.VMEM((nf, 2, SUBLANES, tf), F32)],
        compiler_params=_params(("arbitrary", "arbitrary")),
        name="ffn",
    )(*args)
    return (outs[0], None, None, None) if last else tuple(outs)


def _ple_kernel(x_ref, p_ref, g_ref, wpg_ref, wpe_ref, *rest):
    o_ref = rest[-2] if len(rest) == 3 else rest[-1]
    hn = _normed_bf16(x_ref[...], g_ref)
    pb = p_ref[...].astype(BF16)
    for n in range(D_MODEL // GROUP):
        cols = slice(n * GROUP, (n + 1) * GROUP)
        gate = jax.nn.sigmoid(_dot(hn, wpg_ref[:, cols]))
        o_ref[:, cols] = x_ref[:, cols] + _dot(pb, wpe_ref[:, cols]) * gate
    if len(rest) == 3:
        gnext_ref, _, h_out = rest
        h_out[...] = _normed_bf16(o_ref[...], gnext_ref)


def _ple(l, x, p, g, wpg, wpe, g_mix):
    tm = ROW_TILE
    row = lambda n: pl.BlockSpec((tm, n), lambda i: (i, 0))
    last = l == DEPTH - 1
    in_specs = [row(D_MODEL), pl.BlockSpec((None, None, tm, PLE_DIM), lambda i: (l, 0, i, 0)),
                _lay(g, l), _lay(wpg, l), _lay(wpe, l)]
    out_specs = [row(D_MODEL)]
    out_shape = [jax.ShapeDtypeStruct((SEQ, D_MODEL), F32)]
    args = [x, p, g, wpg, wpe]
    if not last:
        in_specs.append(_lay(g_mix, l + 1))
        out_specs.append(row(D_MODEL))
        out_shape.append(jax.ShapeDtypeStruct((SEQ, D_MODEL), BF16))
        args.append(g_mix)
    outs = pl.pallas_call(
        _ple_kernel,
        grid=(SEQ // tm,),
        in_specs=in_specs,
        out_specs=out_specs,
        out_shape=out_shape,
        compiler_params=_params(("parallel",)),
        name="ple",
    )(*args)
    return (outs[0], None) if last else (outs[0], outs[1])


def _retention_tables():
    c = RET_CHUNK
    lg = jnp.log(1.0 - 2.0 ** (-5.0 - jnp.arange(RET_HEADS, dtype=F32)))
    idx = jnp.arange(c, dtype=F32)
    rel = idx[:, None] - idx[None, :]
    inner = jnp.where(rel[None] >= 0, jnp.exp(lg[:, None, None] * rel[None]), 0.0)
    qdec = jnp.exp(lg[:, None] * (idx + 1.0))[:, :, None]
    kdec_t = jnp.exp(lg[:, None] * (c - 1.0 - idx))
    cdec = jnp.exp(lg * c)
    lane = np.arange(RET_HEADS * RET_DK)
    head_of_lane = lane // RET_DK
    hmask = jnp.asarray(head_of_lane[None, :] == np.arange(RET_HEADS)[:, None], F32)
    qmask = hmask[:, None, :]
    kdec = hmask[:, :, None] * kdec_t[:, None, :]
    cdec_b = jnp.broadcast_to(cdec[:, None, None], (RET_HEADS, 1, RET_DV))
    return qmask, kdec, qdec, inner, cdec_b


def _lane_freq(inv):
    return jnp.tile(inv, LANES // inv.shape[0])[None, :]


def kernel(x, p, positions, g_mix, w_in, g_qa, g_kva, w_q_up, w_kv_up, g_qn, g_kn, w_sc, w_cf,
           b_cf, g_cf_ln, b_cf_ln, g_ret, w_o, g_ffn, w_up, w_ffn_conv, w_down, g_pe, w_pe, w_pg):
    assert x.shape == (1, SEQ, D_MODEL)
    xs = x[0]
    posf = positions[0].astype(F32)[:, None]
    rows = lambda a: a[:, None, :]

    mla_freq = _lane_freq(ROPE_BASE ** (-jnp.arange(0, MLA_ROPE, 2, dtype=F32) / MLA_ROPE))
    ret_freq = _lane_freq(1.0 / (10000.0 ** jnp.linspace(0.0, 1.0, RET_DK // 2, dtype=F32)))
    qmask, kdec, qdec, inner, cdec = _retention_tables()

    w_in_r = jnp.pad(w_in.astype(BF16), ((0, 0), (0, 0), (W_IN_SHIFT, 0)))
    wa = w_in[:, :, :MLA_IN].astype(BF16)
    wq = jnp.pad(w_q_up.reshape(DEPTH, Q_LORA, MLA_HEADS, MLA_QK),
                 ((0, 0), (0, 0), (0, 0), (0, HEAD_PAD - MLA_QK))
                 ).reshape(DEPTH, Q_LORA, MLA_HEADS * HEAD_PAD).astype(BF16)
    wkv4 = w_kv_up.reshape(DEPTH, KV_LORA, MLA_HEADS, MLA_NOPE + MLA_V)
    wkv = jnp.concatenate([wkv4[..., :MLA_NOPE].reshape(DEPTH, KV_LORA, -1),
                           wkv4[..., MLA_NOPE:].reshape(DEPTH, KV_LORA, -1)], axis=2).astype(BF16)
    gq = jnp.pad(g_qn * (MLA_QK ** -0.5 * LOG2_E), ((0, 0), (0, HEAD_PAD - MLA_QK)))
    gkr = jnp.pad(g_kn[:, MLA_NOPE:], ((0, 0), (0, LANES - MLA_ROPE)))
    gm, g_ret4 = rows(g_mix), g_ret[:, :, None, :]

    hm, rope = _prologue(xs, gm, posf, mla_freq, ret_freq)
    for l in range(DEPTH):
        mla = _mla_pre(l, hm, rope, wa, rows(g_qa), rows(g_kva), wq, wkv, rows(gq),
                       rows(g_kn[:, :MLA_NOPE]), rows(gkr), *((w_up, w_down) if l == 0 else ()))
        qt, k, vt = mla[:3]
        if l == 0:
            wg, wu, wd = mla[3:]
        y_a = _flash(qt, k, vt)
        first = l == 0
        y_b, *cast_b = _sconv(l, hm, w_in_r, w_sc, cast=(w_o,) if first else ())
        y_c, *cast_c = _conformer(l, hm, w_in_r, w_cf, rows(b_cf), rows(g_cf_ln), rows(b_cf_ln),
                                  cast=(w_pg, w_pe) if first else ())
        if first:
            (w_o_b,), (w_pg_b, w_pe_b) = cast_b, cast_c
        y_d = _retention(l, hm, rope, w_in_r, qmask, kdec, qdec, inner, cdec, g_ret4)
        xs, hf = _out_proj(l, xs, y_a, y_b, y_c, y_d, w_o_b, rows(g_ffn))
        xs, wg, wu, wd = _ffn(l, xs, hf, wg, wu, wd, w_ffn_conv, w_up, w_down)
        xs, hm = _ple(l, xs, p, rows(g_pe), w_pg_b, w_pe_b, gm)
    return xs[None]
```

```python
import numpy as np
import jax
import jax.numpy as jnp
from jax import lax
from jax.experimental import pallas as pl
from jax.experimental.pallas import tpu as pltpu

F32 = jnp.float32
BF16 = jnp.bfloat16

D_MODEL = 2048
SEQ = 8192
DEPTH = 2
GROUP = 512
MLA_HEADS = 4
MLA_NOPE = 128
MLA_ROPE = 64
MLA_V = 128
MLA_QK = MLA_NOPE + MLA_ROPE
Q_LORA = 512
KV_LORA = 256
ROPE_BASE = 10000.0
SC_K = 3
CF_K = 31
RET_HEADS = 4
RET_DV = 128
RET_DK = 64
RET_CHUNK = 128
D_FF = 5632
FFN_K = 3
PLE_DIM = 256
EPS = 1e-6
N_IN = Q_LORA + KV_LORA + MLA_ROPE + 3 * GROUP + 2 * GROUP + 2 * RET_HEADS * (RET_DK + RET_DV)

LANES = 128
SUBLANES = 8
HEAD_PAD = 2 * LANES
ROW_TILE = 512
FF_TILE = 512
FF_SUB = 2
DOT_ROWS = 128
ATT_TILE = 512
ATT_HEADS_PER_STEP = 4
V_ONES = 16
CF_HALO = 32
CF_ROWS = 128
MLA_IN = Q_LORA + KV_LORA + LANES
W_IN_SHIFT = 3 * GROUP - (Q_LORA + KV_LORA + MLA_ROPE)
W_IN_GROUPS = {"sconv": (3 * GROUP, 1), "conformer": (2 * GROUP, 3), "ret_qk": (GROUP, 8),
               "ret_v": (GROUP, 9), "ret_gate": (GROUP, 10)}
VMEM_LIMIT = 56 * 1024 * 1024
MASK_VALUE = -1e30
LOG2_E = 1.4426950408889634


def _lay(a, l):
    return pl.BlockSpec((None,) + a.shape[1:], lambda *_: (l,) + (0,) * (a.ndim - 1))


def _const(a):
    return pl.BlockSpec(a.shape, lambda *_: (0,) * a.ndim)


def _w_in_spec(l, group):
    width, block = W_IN_GROUPS[group]
    return pl.BlockSpec((None, D_MODEL, width), lambda *_: (l, 0, block))


def _params(sem):
    return pltpu.CompilerParams(dimension_semantics=sem, vmem_limit_bytes=VMEM_LIMIT)


def _rms_scale(xf, n):
    return lax.rsqrt(jnp.sum(xf * xf, axis=-1, keepdims=True) * (1.0 / n) + EPS)


def _normed_bf16(xf, g_ref):
    return (xf * _rms_scale(xf, D_MODEL) * g_ref[...]).astype(BF16)


def _with_casts(body, n_in, n_out, n_cast):
    def kernel_fn(*refs):
        ins, srcs = refs[:n_in], refs[n_in:n_in + n_cast]
        outs = refs[n_in + n_cast:n_in + n_cast + n_out]
        dsts = refs[n_in + n_cast + n_out:n_in + 2 * n_cast + n_out]
        for src, dst in zip(srcs, dsts):
            dst[...] = src[...].astype(BF16)
        body(*ins, *outs, *refs[n_in + 2 * n_cast + n_out:])
    return kernel_fn


def _cast_specs(items, steps):
    arrays, ins, outs, shapes = [], [], [], []
    for item in items:
        if isinstance(item, tuple):
            a, layer, ncol, col = item
            rows, cols = a.shape[-2] // steps, a.shape[-1] // ncol
            ins.append(pl.BlockSpec((None, rows, cols), lambda i, layer=layer, col=col: (layer, i, col)))
            outs.append(pl.BlockSpec((rows, cols), lambda i: (i, 0)))
            shapes.append(jax.ShapeDtypeStruct((a.shape[-2], cols), BF16))
        else:
            a = item
            blk = a.shape[:-2] + (a.shape[-2] // steps, a.shape[-1])
            idx = lambda i, nd=a.ndim: (0,) * (nd - 2) + (i, 0)
            ins.append(pl.BlockSpec(blk, idx))
            outs.append(pl.BlockSpec(blk, idx))
            shapes.append(jax.ShapeDtypeStruct(a.shape, BF16))
        arrays.append(a)
    return arrays, ins, outs, shapes


def _dot(a, b):
    return jnp.dot(a, b, preferred_element_type=F32)


def _rdot(a, b):
    n = a.shape[0] // DOT_ROWS
    return jnp.concatenate([_dot(a[r * DOT_ROWS:(r + 1) * DOT_ROWS], b) for r in range(n)], axis=0)


def _prologue_kernel(x_ref, g_ref, pos_ref, fm_ref, fr_ref, h_out, o_ref):
    h_out[...] = _normed_bf16(x_ref[...], g_ref)
    lane = lax.broadcasted_iota(jnp.int32, (pos_ref.shape[0], LANES), 1)
    pos = pos_ref[...]
    for n, (f_ref, is_x1, is_x2) in enumerate((
            (fm_ref, lane < MLA_ROPE // 2, (lane >= MLA_ROPE // 2) & (lane < MLA_ROPE)),
            (fr_ref, (lane % RET_DK) < RET_DK // 2, (lane % RET_DK) >= RET_DK // 2))):
        ang = pos * f_ref[...]
        sin_t = jnp.sin(ang)
        o_ref[:, (3 * n) * LANES:(3 * n + 1) * LANES] = jnp.cos(ang)
        o_ref[:, (3 * n + 1) * LANES:(3 * n + 2) * LANES] = jnp.where(is_x1, -sin_t, 0.0)
        o_ref[:, (3 * n + 2) * LANES:(3 * n + 3) * LANES] = jnp.where(is_x2, sin_t, 0.0)


def _prologue(x, g_mix, posf, mla_freq, ret_freq):
    tm = ROW_TILE
    row = pl.BlockSpec((tm, D_MODEL), lambda i: (i, 0))
    return pl.pallas_call(
        _prologue_kernel,
        grid=(SEQ // tm,),
        in_specs=[row, _lay(g_mix, 0), pl.BlockSpec((tm, 1), lambda i: (i, 0)), _const(mla_freq),
                  _const(ret_freq)],
        out_specs=[row, pl.BlockSpec((tm, 6 * LANES), lambda i: (i, 0))],
        out_shape=[jax.ShapeDtypeStruct((SEQ, D_MODEL), BF16), jax.ShapeDtypeStruct((SEQ, 6 * LANES), F32)],
        compiler_params=_params(("parallel",)),
        name="prologue",
    )(x, g_mix, posf, mla_freq, ret_freq)


def _rope_spec(which):
    return pl.BlockSpec((ROW_TILE, 3 * LANES), lambda i: (i, which))


def _rope_slab(x, cos_t, sin_a, sin_b):
    return (x * cos_t + pltpu.roll(x, LANES - 32, axis=1) * sin_a
            + pltpu.roll(x, 32, axis=1) * sin_b)


def _mla_pre_kernel(h_ref, rope_ref, wa_ref, gqa_ref, gkva_ref, wq_ref, wkv_ref,
                    gq_ref, gkn_ref, gkr_ref, qt_out, k_out, vt_out):
    z = _rdot(h_ref[...], wa_ref[...])
    zq = z[:, :Q_LORA]
    zkv = z[:, Q_LORA:Q_LORA + KV_LORA]
    lane = lax.broadcasted_iota(jnp.int32, (z.shape[0], LANES), 1)
    zkr = jnp.where(lane < MLA_ROPE, z[:, Q_LORA + KV_LORA:], 0.0)
    c_q = (zq * _rms_scale(zq, Q_LORA) * gqa_ref[...]).astype(BF16)
    c_kv = (zkv * _rms_scale(zkv, KV_LORA) * gkva_ref[...]).astype(BF16)

    cos_t, sin_a, sin_b = (rope_ref[:, n * LANES:(n + 1) * LANES] for n in range(3))

    q = _rdot(c_q, wq_ref[...])
    kv = _rdot(c_kv, wkv_ref[...])

    kr = zkr * gkr_ref[...]
    ss_kr = jnp.sum(zkr * zkr, axis=-1, keepdims=True)
    kr_rot = _rope_slab(kr, cos_t, sin_a, sin_b)
    ones = jnp.ones((V_ONES, z.shape[0]), BF16)
    for hd in range(MLA_HEADS):
        qb = q[:, hd * HEAD_PAD:(hd + 1) * HEAD_PAD]
        qn = qb * _rms_scale(qb, MLA_QK) * gq_ref[...]
        qt_out[hd, 0:LANES, :] = qn[:, :LANES].T.astype(BF16)
        qt_out[hd, LANES:, :] = _rope_slab(qn[:, LANES:], cos_t, sin_a, sin_b).T.astype(BF16)
        vh = kv[:, MLA_HEADS * MLA_NOPE + hd * MLA_V:MLA_HEADS * MLA_NOPE + (hd + 1) * MLA_V]
        vt_out[hd, 0, 0:MLA_V, :] = vh.T.astype(BF16)
        vt_out[hd, 0, MLA_V:, :] = ones
        kn = kv[:, hd * MLA_NOPE:(hd + 1) * MLA_NOPE]
        r_k = lax.rsqrt((jnp.sum(kn * kn, axis=-1, keepdims=True) + ss_kr) * (1.0 / MLA_QK) + EPS)
        k_out[:, hd * HEAD_PAD:hd * HEAD_PAD + LANES] = (kn * r_k * gkn_ref[...]).astype(BF16)
        k_out[:, hd * HEAD_PAD + LANES:(hd + 1) * HEAD_PAD] = (kr_rot * r_k).astype(BF16)


def _mla_pre(l, h, rope, wa, g_qa, g_kva, wq, wkv, gq, gkn, gkr):
    tm = ROW_TILE
    row = lambda w: pl.BlockSpec((tm, w), lambda i: (i, 0))
    return pl.pallas_call(
        _mla_pre_kernel,
        grid=(SEQ // tm,),
        in_specs=[row(D_MODEL), _rope_spec(0)]
        + [_lay(a, l) for a in (wa, g_qa, g_kva, wq, wkv, gq, gkn, gkr)],
        out_specs=[pl.BlockSpec((MLA_HEADS, HEAD_PAD, tm), lambda i: (0, 0, i)),
                   row(MLA_HEADS * HEAD_PAD),
                   pl.BlockSpec((MLA_HEADS, 1, MLA_V + V_ONES, tm), lambda i: (0, i, 0, 0))],
        out_shape=[jax.ShapeDtypeStruct((MLA_HEADS, HEAD_PAD, SEQ), BF16),
                   jax.ShapeDtypeStruct((SEQ, MLA_HEADS * HEAD_PAD), BF16),
                   jax.ShapeDtypeStruct((MLA_HEADS, SEQ // tm, MLA_V + V_ONES, tm), BF16)],
        compiler_params=_params(("parallel",)),
        name="mla_pre",
    )(h, rope, wa, g_qa, g_kva, wq, wkv, gq, gkn, gkr)


def _flash_kernel(qt_ref, k_ref, vt_ref, o_ref, m_scr, acc_scr, st_a, st_b):
    t = ATT_TILE
    i = pl.program_id(1)
    sub = SUBLANES
    pack = 2 * SUBLANES
    vrows = MLA_V + V_ONES
    m_scr[...] = jnp.full(m_scr.shape, MASK_VALUE, F32)
    acc_scr[...] = jnp.zeros(acc_scr.shape, F32)

    def logits(j, dst):
        start = pl.multiple_of(j * t, t)
        for hh in range(ATT_HEADS_PER_STEP):
            dst[hh] = _dot(k_ref[pl.ds(start, t), hh * HEAD_PAD:(hh + 1) * HEAD_PAD], qt_ref[hh])

    def softmax_pv(src, j, masked):
        if masked:
            key = lax.broadcasted_iota(jnp.int32, (t, t), 0)
            qry = lax.broadcasted_iota(jnp.int32, (t, t), 1)
            keep = key <= qry
        for hh in range(ATT_HEADS_PER_STEP):
            st = src[hh]
            if masked:
                st = jnp.where(keep, st, MASK_VALUE)
            s3 = st.reshape(t // pack, pack, t)
            mx = jnp.max(s3, axis=0)
            mx = jnp.maximum(mx[:sub], mx[sub:])
            for shift in (4, 2, 1):
                mx = jnp.maximum(mx, pltpu.roll(mx, shift, axis=0))
            m_prev = m_scr[hh]
            m_new = jnp.maximum(m_prev, mx)
            alpha = jnp.exp2(m_prev - m_new)
            m2 = jnp.concatenate([m_new, m_new], axis=0)
            p = jnp.exp2((s3 - m2[None]).astype(BF16)).reshape(t, t)
            pv = _dot(vt_ref[hh, j], p)
            acc = acc_scr[hh].reshape(vrows // sub, sub, t) * alpha[None] + pv.reshape(vrows // sub, sub, t)
            acc_scr[hh] = acc.reshape(vrows, t)
            m_scr[hh] = m_new

    logits(0, st_a)

    def pair(m, carry):
        logits(2 * m + 1, st_b)
        softmax_pv(st_a, 2 * m, False)
        logits(2 * m + 2, st_a)
        softmax_pv(st_b, 2 * m + 1, False)
        return carry

    lax.fori_loop(0, lax.shift_right_logical(i, 1), pair, 0)
    odd = (i & 1) == 1

    @pl.when(odd)
    def _():
        logits(i, st_b)
        softmax_pv(st_a, i - 1, False)
        softmax_pv(st_b, i, True)

    @pl.when(jnp.logical_not(odd))
    def _():
        softmax_pv(st_a, i, True)

    for hh in range(ATT_HEADS_PER_STEP):
        num = acc_scr[hh, 0:MLA_V, :].reshape(MLA_V // sub, sub, t)
        den = acc_scr[hh, MLA_V:MLA_V + sub, :]
        out_t = (num / den[None]).reshape(MLA_V, t)
        o_ref[:, hh * MLA_V:(hh + 1) * MLA_V] = out_t.T.astype(o_ref.dtype)


def _flash(qt, k, vt):
    t, hp = ATT_TILE, ATT_HEADS_PER_STEP
    vrows = MLA_V + V_ONES
    return pl.pallas_call(
        _flash_kernel,
        grid=(MLA_HEADS // hp, SEQ // t),
        in_specs=[pl.BlockSpec((hp, HEAD_PAD, t), lambda h, i: (h, 0, i)),
                  pl.BlockSpec((SEQ, hp * HEAD_PAD), lambda h, i: (0, h), pipeline_mode=pl.Buffered(1)),
                  pl.BlockSpec((hp, SEQ // t, vrows, t), lambda h, i: (h, 0, 0, 0),
                               pipeline_mode=pl.Buffered(1))],
        out_specs=pl.BlockSpec((t, hp * MLA_V), lambda h, i: (i, h)),
        out_shape=jax.ShapeDtypeStruct((SEQ, MLA_HEADS * MLA_V), BF16),
        scratch_shapes=[pltpu.VMEM((hp, SUBLANES, t), F32), pltpu.VMEM((hp, vrows, t), F32),
                        pltpu.VMEM((hp, t, t), F32), pltpu.VMEM((hp, t, t), F32)],
        compiler_params=_params(("parallel", "arbitrary")),
        name="flash",
    )(qt, k, vt)


def _sconv_kernel(h_ref, w_ref, wsc_ref, o_ref, buf):
    tm = ROW_TILE
    z = _rdot(h_ref[...], w_ref[...])

    @pl.when(pl.program_id(0) == 0)
    def _():
        buf[0:SUBLANES, :] = jnp.zeros((SUBLANES, GROUP), F32)

    g = z[:, GROUP:2 * GROUP] * z[:, 2 * GROUP:]
    buf[SUBLANES:, :] = g
    w = wsc_ref[...]
    conv = (w[2:3, :] * g + w[1:2, :] * buf[SUBLANES - 1:SUBLANES - 1 + tm, :]
            + w[0:1, :] * buf[SUBLANES - 2:SUBLANES - 2 + tm, :])
    o_ref[...] = (z[:, :GROUP] * conv).astype(o_ref.dtype)
    buf[0:SUBLANES, :] = buf[tm:tm + SUBLANES, :]


def _sconv(l, h, w_in, wsc, cast=()):
    tm = ROW_TILE
    n = SEQ // tm
    c_arr, c_in, c_out, c_shape = _cast_specs(cast, n)
    return pl.pallas_call(
        _with_casts(_sconv_kernel, 3, 1, len(cast)),
        grid=(n,),
        in_specs=[pl.BlockSpec((tm, D_MODEL), lambda i: (i, 0)),
                  _w_in_spec(l, "sconv"), _lay(wsc, l)] + c_in,
        out_specs=[pl.BlockSpec((tm, GROUP), lambda i: (i, 0))] + c_out,
        out_shape=[jax.ShapeDtypeStruct((SEQ, GROUP), BF16)] + c_shape,
        scratch_shapes=[pltpu.VMEM((tm + SUBLANES, GROUP), F32)],
        compiler_params=_params(("arbitrary",)),
        name="sconv",
    )(h, w_in, wsc, *c_arr)


def _conformer_kernel(h_ref, w_ref, wcf_ref, bcf_ref, gln_ref, bln_ref, o_ref, buf, wbuf, uscr):
    tm = ROW_TILE
    nslab = GROUP // LANES
    z = _rdot(h_ref[...], w_ref[...])

    @pl.when(pl.program_id(0) == 0)
    def _():
        buf[:, 0:CF_HALO, :] = jnp.zeros((nslab, CF_HALO, LANES), F32)

    glu = z[:, :GROUP] * jax.nn.sigmoid(z[:, GROUP:])
    for sl in range(nslab):
        buf[sl, CF_HALO:, :] = glu[:, sl * LANES:(sl + 1) * LANES]
    base = CF_HALO - (CF_K - 1)
    for phase in range(1, SUBLANES):
        wbuf[phase - 1] = buf[:, phase:phase + tm + CF_HALO - SUBLANES, :]

    def tap_rows(o, r0, sl):
        phase = o % SUBLANES
        start = pl.multiple_of(r0 + (o - phase), SUBLANES)
        if phase == 0:
            return buf[sl, pl.ds(start, CF_ROWS), :]
        return wbuf[phase - 1, sl, pl.ds(start, CF_ROWS), :]

    def row_block(rb, carry):
        r0 = pl.multiple_of(rb * CF_ROWS, CF_ROWS)
        for sl in range(nslab):
            lanes = slice(sl * LANES, (sl + 1) * LANES)
            acc = jnp.broadcast_to(bcf_ref[:, lanes], (CF_ROWS, LANES))
            for j in range(CF_K):
                acc = acc + wcf_ref[j:j + 1, lanes] * tap_rows(base + j, r0, sl)
            uscr[pl.ds(r0, CF_ROWS), lanes] = acc
        return carry

    lax.fori_loop(0, tm // CF_ROWS, row_block, 0)
    u = uscr[...]
    mu = jnp.mean(u, axis=-1, keepdims=True)
    uc = u - mu
    y = uc * lax.rsqrt(jnp.mean(uc * uc, axis=-1, keepdims=True) + EPS)
    y = y * gln_ref[...] + bln_ref[...]
    o_ref[...] = (y * jax.nn.sigmoid(y)).astype(o_ref.dtype)
    buf[:, 0:CF_HALO, :] = buf[:, tm:tm + CF_HALO, :]


def _conformer(l, h, w_in, wcf, bcf, gln, bln, cast=()):
    tm = ROW_TILE
    n = SEQ // tm
    c_arr, c_in, c_out, c_shape = _cast_specs(cast, n)
    return pl.pallas_call(
        _with_casts(_conformer_kernel, 6, 1, len(cast)),
        grid=(n,),
        in_specs=[pl.BlockSpec((tm, D_MODEL), lambda i: (i, 0)), _w_in_spec(l, "conformer")]
        + [_lay(a, l) for a in (wcf, bcf, gln, bln)] + c_in,
        out_specs=[pl.BlockSpec((tm, GROUP), lambda i: (i, 0))] + c_out,
        out_shape=[jax.ShapeDtypeStruct((SEQ, GROUP), BF16)] + c_shape,
        scratch_shapes=[pltpu.VMEM((GROUP // LANES, tm + CF_HALO, LANES), F32),
                        pltpu.VMEM((SUBLANES - 1, GROUP // LANES, tm + CF_HALO - SUBLANES, LANES), F32),
                        pltpu.VMEM((tm, GROUP), F32)],
        compiler_params=_params(("arbitrary",)),
        name="conformer",
    )(h, w_in, wcf, bcf, gln, bln, *c_arr)


def _retention_kernel(h_ref, rope_ref, wqk_ref, wv_ref, wg_ref, qmask_ref, kdec_ref, qdec_ref,
                      inner_ref, cdec_ref, gret_ref, o_ref, state):
    tm = ROW_TILE
    c = RET_CHUNK
    dk2 = RET_HEADS * RET_DK

    @pl.when(pl.program_id(0) == 0)
    def _():
        state[...] = jnp.zeros(state.shape, F32)

    half = RET_DK // 2

    for ci in range(tm // c):
        rows = slice(ci * c, (ci + 1) * c)
        hb = h_ref[rows, :]
        zqk = _dot(hb, wqk_ref[...])
        v = _dot(hb, wv_ref[...])
        gate = _dot(hb, wg_ref[...])
        cos_t, sin_a, sin_b = (rope_ref[rows, n * LANES:(n + 1) * LANES] for n in range(3))

        def rot(t):
            return jnp.concatenate(
                [x * cos_t + pltpu.roll(x, LANES - half, axis=1) * sin_a + pltpu.roll(x, half, axis=1) * sin_b
                 for x in (t[:, :LANES], t[:, LANES:])], axis=1)

        qc = rot(zqk[:, :dk2])
        kt = (rot(zqk[:, dk2:]) * (RET_DK ** -0.5)).T
        kt_b = kt.astype(BF16)
        for hd in range(RET_HEADS):
            cols = slice(hd * RET_DV, (hd + 1) * RET_DV)
            qm = (qc * qmask_ref[hd]).astype(BF16)
            vh = v[:, cols].astype(BF16)
            sc = _dot(qm, kt_b) * inner_ref[hd]
            st = state[hd]
            o = _dot(sc.astype(BF16), vh) + _dot(qm, st.astype(BF16)) * qdec_ref[hd]
            state[hd] = st * cdec_ref[hd] + _dot((kt * kdec_ref[hd]).astype(BF16), vh)
            on = o * _rms_scale(o, RET_DV) * gret_ref[hd]
            gh = gate[:, cols]
            o_ref[rows, cols] = (gh * jax.nn.sigmoid(gh) * on).astype(o_ref.dtype)


def _retention(l, h, rope, w_in, qmask, kdec, qdec, inner, cdec, gret):
    tm = ROW_TILE
    return pl.pallas_call(
        _retention_kernel,
        grid=(SEQ // tm,),
        in_specs=[pl.BlockSpec((tm, D_MODEL), lambda i: (i, 0)), _rope_spec(1), _w_in_spec(l, "ret_qk"),
                  _w_in_spec(l, "ret_v"), _w_in_spec(l, "ret_gate")]
        + [_const(a) for a in (qmask, kdec, qdec, inner, cdec)] + [_lay(gret, l)],
        out_specs=pl.BlockSpec((tm, GROUP), lambda i: (i, 0)),
        out_shape=jax.ShapeDtypeStruct((SEQ, GROUP), BF16),
        scratch_shapes=[pltpu.VMEM((RET_HEADS, RET_HEADS * RET_DK, RET_DV), F32)],
        compiler_params=_params(("arbitrary",)),
        name="retention",
    )(h, rope, w_in, w_in, w_in, qmask, kdec, qdec, inner, cdec, gret)


def _out_proj_kernel(x_ref, ya_ref, yb_ref, yc_ref, yd_ref, wo_ref, g_ref, o_ref, h_out):
    acc = x_ref[...]
    for n, y_ref in enumerate((ya_ref, yb_ref, yc_ref, yd_ref)):
        acc = acc + _dot(y_ref[...], wo_ref[n * GROUP:(n + 1) * GROUP, :])
    o_ref[...] = acc
    h_out[...] = _normed_bf16(acc, g_ref)


def _out_proj(l, x, ya, yb, yc, yd, wo, g_ffn):
    tm = ROW_TILE
    row = lambda n: pl.BlockSpec((tm, n), lambda i: (i, 0))
    return pl.pallas_call(
        _out_proj_kernel,
        grid=(SEQ // tm,),
        in_specs=[row(D_MODEL), row(GROUP), row(GROUP), row(GROUP), row(GROUP), _lay(wo, l),
                  _lay(g_ffn, l)],
        out_specs=[row(D_MODEL), row(D_MODEL)],
        out_shape=[jax.ShapeDtypeStruct((SEQ, D_MODEL), F32), jax.ShapeDtypeStruct((SEQ, D_MODEL), BF16)],
        compiler_params=_params(("parallel",)),
        name="out_proj",
    )(x, ya, yb, yc, yd, wo, g_ffn)


def _ffn_kernel(x_ref, h_ref, wg_ref, wu_ref, cg_ref, cu_ref, wd_ref, o_ref, ubuf, carry):
    tm = ROW_TILE
    i = pl.program_id(0)
    j = pl.program_id(1)

    @pl.when(j == 0)
    def _():
        o_ref[...] = x_ref[...]

    @pl.when(i == 0)
    def _():
        carry[j] = jnp.zeros(carry.shape[1:], F32)

    ts = FF_TILE // FF_SUB
    h = h_ref[...]

    def conv(part, s, u, c_ref):
        cols = slice(s * ts, (s + 1) * ts)
        ubuf[part, s, 0:SUBLANES, :] = carry[j, part, :, cols]
        ubuf[part, s, SUBLANES:, :] = u
        carry[j, part, :, cols] = u[tm - SUBLANES:, :]
        w = c_ref[:, cols]
        return (w[2:3, :] * u + w[1:2, :] * ubuf[part, s, SUBLANES - 1:SUBLANES - 1 + tm, :]
                + w[0:1, :] * ubuf[part, s, SUBLANES - 2:SUBLANES - 2 + tm, :])

    us = [(_rdot(h, wg_ref[:, s * ts:(s + 1) * ts]), _rdot(h, wu_ref[:, s * ts:(s + 1) * ts]))
          for s in range(FF_SUB)]
    for s, (ug, uu) in enumerate(us):
        gate = conv(0, s, ug, cg_ref)
        up = conv(1, s, uu, cu_ref)
        act = (gate * jax.nn.sigmoid(gate) * up).astype(BF16)
        o_ref[...] += _dot(act, wd_ref[s * ts:(s + 1) * ts, :])


def _ffn(l, x, h, wg, wu, wd, w_conv):
    tm, tf = ROW_TILE, FF_TILE
    nf = D_FF // tf
    return pl.pallas_call(
        _ffn_kernel,
        grid=(SEQ // tm, nf),
        in_specs=[pl.BlockSpec((tm, D_MODEL), lambda i, j: (i, 0)),
                  pl.BlockSpec((tm, D_MODEL), lambda i, j: (i, 0)),
                  pl.BlockSpec((D_MODEL, tf), lambda i, j: (0, j)),
                  pl.BlockSpec((D_MODEL, tf), lambda i, j: (0, j)),
                  pl.BlockSpec((None, FFN_K, tf), lambda i, j: (l, 0, j)),
                  pl.BlockSpec((None, FFN_K, tf), lambda i, j: (l, 0, j + nf)),
                  pl.BlockSpec((tf, D_MODEL), lambda i, j: (j, 0))],
        out_specs=pl.BlockSpec((tm, D_MODEL), lambda i, j: (i, 0)),
        out_shape=jax.ShapeDtypeStruct((SEQ, D_MODEL), F32),
        scratch_shapes=[pltpu.VMEM((2, FF_SUB, tm + SUBLANES, tf // FF_SUB), F32),
                        pltpu.VMEM((nf, 2, SUBLANES, tf), F32)],
        compiler_params=_params(("arbitrary", "arbitrary")),
        name="ffn",
    )(x, h, wg, wu, w_conv, w_conv, wd)


def _ple_kernel(x_ref, p_ref, g_ref, wpg_ref, wpe_ref, *rest):
    o_ref = rest[-2] if len(rest) == 3 else rest[-1]
    hn = _normed_bf16(x_ref[...], g_ref)
    pb = p_ref[...].astype(BF16)
    for n in range(D_MODEL // GROUP):
        cols = slice(n * GROUP, (n + 1) * GROUP)
        gate = jax.nn.sigmoid(_dot(hn, wpg_ref[:, cols]))
        o_ref[:, cols] = x_ref[:, cols] + _dot(pb, wpe_ref[:, cols]) * gate
    if len(rest) == 3:
        gnext_ref, _, h_out = rest
        h_out[...] = _normed_bf16(o_ref[...], gnext_ref)


def _ple(l, x, p, g, wpg, wpe, g_mix):
    tm = ROW_TILE
    row = lambda n: pl.BlockSpec((tm, n), lambda i: (i, 0))
    last = l == DEPTH - 1
    in_specs = [row(D_MODEL), pl.BlockSpec((None, None, tm, PLE_DIM), lambda i: (l, 0, i, 0)),
                _lay(g, l), _lay(wpg, l), _lay(wpe, l)]
    out_specs = [row(D_MODEL)]
    out_shape = [jax.ShapeDtypeStruct((SEQ, D_MODEL), F32)]
    args = [x, p, g, wpg, wpe]
    if not last:
        in_specs.append(_lay(g_mix, l + 1))
        out_specs.append(row(D_MODEL))
        out_shape.append(jax.ShapeDtypeStruct((SEQ, D_MODEL), BF16))
        args.append(g_mix)
    outs = pl.pallas_call(
        _ple_kernel,
        grid=(SEQ // tm,),
        in_specs=in_specs,
        out_specs=out_specs,
        out_shape=out_shape,
        compiler_params=_params(("parallel",)),
        name="ple",
    )(*args)
    return (outs[0], None) if last else (outs[0], outs[1])


def _retention_tables():
    c = RET_CHUNK
    lg = jnp.log(1.0 - 2.0 ** (-5.0 - jnp.arange(RET_HEADS, dtype=F32)))
    idx = jnp.arange(c, dtype=F32)
    rel = idx[:, None] - idx[None, :]
    inner = jnp.where(rel[None] >= 0, jnp.exp(lg[:, None, None] * rel[None]), 0.0)
    qdec = jnp.exp(lg[:, None] * (idx + 1.0))[:, :, None]
    kdec_t = jnp.exp(lg[:, None] * (c - 1.0 - idx))
    cdec = jnp.exp(lg * c)
    lane = np.arange(RET_HEADS * RET_DK)
    head_of_lane = lane // RET_DK
    hmask = jnp.asarray(head_of_lane[None, :] == np.arange(RET_HEADS)[:, None], F32)
    qmask = hmask[:, None, :]
    kdec = hmask[:, :, None] * kdec_t[:, None, :]
    cdec_b = jnp.broadcast_to(cdec[:, None, None], (RET_HEADS, 1, RET_DV))
    return qmask, kdec, qdec, inner, cdec_b


def _lane_freq(inv):
    return jnp.tile(inv, LANES // inv.shape[0])[None, :]


def kernel(x, p, positions, g_mix, w_in, g_qa, g_kva, w_q_up, w_kv_up, g_qn, g_kn, w_sc, w_cf,
           b_cf, g_cf_ln, b_cf_ln, g_ret, w_o, g_ffn, w_up, w_ffn_conv, w_down, g_pe, w_pe, w_pg):
    assert x.shape == (1, SEQ, D_MODEL)
    xs = x[0]
    posf = positions[0].astype(F32)[:, None]
    rows = lambda a: a[:, None, :]

    mla_freq = _lane_freq(ROPE_BASE ** (-jnp.arange(0, MLA_ROPE, 2, dtype=F32) / MLA_ROPE))
    ret_freq = _lane_freq(1.0 / (10000.0 ** jnp.linspace(0.0, 1.0, RET_DK // 2, dtype=F32)))
    qmask, kdec, qdec, inner, cdec = _retention_tables()

    w_in_r = jnp.pad(w_in.astype(BF16), ((0, 0), (0, 0), (W_IN_SHIFT, 0)))
    wa = w_in[:, :, :MLA_IN].astype(BF16)
    wq = jnp.pad(w_q_up.reshape(DEPTH, Q_LORA, MLA_HEADS, MLA_QK),
                 ((0, 0), (0, 0), (0, 0), (0, HEAD_PAD - MLA_QK))
                 ).reshape(DEPTH, Q_LORA, MLA_HEADS * HEAD_PAD).astype(BF16)
    wkv4 = w_kv_up.reshape(DEPTH, KV_LORA, MLA_HEADS, MLA_NOPE + MLA_V)
    wkv = jnp.concatenate([wkv4[..., :MLA_NOPE].reshape(DEPTH, KV_LORA, -1),
                           wkv4[..., MLA_NOPE:].reshape(DEPTH, KV_LORA, -1)], axis=2).astype(BF16)
    gq = jnp.pad(g_qn * (MLA_QK ** -0.5 * LOG2_E), ((0, 0), (0, HEAD_PAD - MLA_QK)))
    gkr = jnp.pad(g_kn[:, MLA_NOPE:], ((0, 0), (0, LANES - MLA_ROPE)))
    gm, g_ret4 = rows(g_mix), g_ret[:, :, None, :]

    hm, rope = _prologue(xs, gm, posf, mla_freq, ret_freq)
    for l in range(DEPTH):
        qt, k, vt = _mla_pre(l, hm, rope, wa, rows(g_qa), rows(g_kva), wq, wkv, rows(gq),
                             rows(g_kn[:, :MLA_NOPE]), rows(gkr))
        y_a = _flash(qt, k, vt)
        first = l == 0
        y_b, wg, wu, wd, *cast_b = _sconv(
            l, hm, w_in_r, w_sc,
            cast=((w_up, l, 2, 0), (w_up, l, 2, 1), (w_down, l, 1, 0)) + ((w_o,) if first else ()))
        y_c, *cast_c = _conformer(l, hm, w_in_r, w_cf, rows(b_cf), rows(g_cf_ln), rows(b_cf_ln),
                                  cast=(w_pg, w_pe) if first else ())
        if first:
            (w_o_b,), (w_pg_b, w_pe_b) = cast_b, cast_c
        y_d = _retention(l, hm, rope, w_in_r, qmask, kdec, qdec, inner, cdec, g_ret4)
        xs, hf = _out_proj(l, xs, y_a, y_b, y_c, y_d, w_o_b, rows(g_ffn))
        xs = _ffn(l, xs, hf, wg, wu, wd, w_ffn_conv)
        xs, hm = _ple(l, xs, p, rows(g_pe), w_pg_b, w_pe_b, gm)
    return xs[None]
```

```python
import numpy as np
import jax
import jax.numpy as jnp
from jax import lax
from jax.experimental import pallas as pl
from jax.experimental.pallas import tpu as pltpu

F32 = jnp.float32
BF16 = jnp.bfloat16

D_MODEL = 2048
SEQ = 8192
DEPTH = 2
GROUP = 512
MLA_HEADS = 4
MLA_NOPE = 128
MLA_ROPE = 64
MLA_V = 128
MLA_QK = MLA_NOPE + MLA_ROPE
Q_LORA = 512
KV_LORA = 256
ROPE_BASE = 10000.0
SC_K = 3
CF_K = 31
RET_HEADS = 4
RET_DV = 128
RET_DK = 64
RET_CHUNK = 128
D_FF = 5632
FFN_K = 3
PLE_DIM = 256
EPS = 1e-6
N_IN = Q_LORA + KV_LORA + MLA_ROPE + 3 * GROUP + 2 * GROUP + 2 * RET_HEADS * (RET_DK + RET_DV)

LANES = 128
SUBLANES = 8
HEAD_PAD = 2 * LANES
ROW_TILE = 512
FF_TILE = 512
FF_SUB = 2
DOT_ROWS = 128
ATT_TILE = 512
ATT_HEADS_PER_STEP = 4
V_ONES = 16
CF_HALO = 32
CF_ROWS = 128
MLA_IN = Q_LORA + KV_LORA + LANES
W_IN_SHIFT = 3 * GROUP - (Q_LORA + KV_LORA + MLA_ROPE)
W_IN_GROUPS = {"sconv": (3 * GROUP, 1), "conformer": (2 * GROUP, 3), "ret_qk": (GROUP, 8),
               "ret_v": (GROUP, 9), "ret_gate": (GROUP, 10)}
VMEM_LIMIT = 56 * 1024 * 1024
MASK_VALUE = -1e30
LOG2_E = 1.4426950408889634


def _lay(a, l):
    return pl.BlockSpec((None,) + a.shape[1:], lambda *_: (l,) + (0,) * (a.ndim - 1))


def _const(a):
    return pl.BlockSpec(a.shape, lambda *_: (0,) * a.ndim)


def _w_in_spec(l, group):
    width, block = W_IN_GROUPS[group]
    return pl.BlockSpec((None, D_MODEL, width), lambda *_: (l, 0, block))


def _params(sem):
    return pltpu.CompilerParams(dimension_semantics=sem, vmem_limit_bytes=VMEM_LIMIT)


def _rms_scale(xf, n):
    return lax.rsqrt(jnp.sum(xf * xf, axis=-1, keepdims=True) * (1.0 / n) + EPS)


def _normed_bf16(xf, g_ref):
    return (xf * _rms_scale(xf, D_MODEL) * g_ref[...]).astype(BF16)


def _with_casts(body, n_in, n_out, n_cast):
    def kernel_fn(*refs):
        ins, srcs = refs[:n_in], refs[n_in:n_in + n_cast]
        outs = refs[n_in + n_cast:n_in + n_cast + n_out]
        dsts = refs[n_in + n_cast + n_out:n_in + 2 * n_cast + n_out]
        for src, dst in zip(srcs, dsts):
            dst[...] = src[...].astype(BF16)
        body(*ins, *outs, *refs[n_in + 2 * n_cast + n_out:])
    return kernel_fn


def _cast_specs(items, steps):
    arrays, ins, outs, shapes = [], [], [], []
    for item in items:
        if isinstance(item, tuple):
            a, layer, ncol, col = item
            rows, cols = a.shape[-2] // steps, a.shape[-1] // ncol
            ins.append(pl.BlockSpec((None, rows, cols), lambda i, layer=layer, col=col: (layer, i, col)))
            outs.append(pl.BlockSpec((rows, cols), lambda i: (i, 0)))
            shapes.append(jax.ShapeDtypeStruct((a.shape[-2], cols), BF16))
        else:
            a = item
            blk = a.shape[:-2] + (a.shape[-2] // steps, a.shape[-1])
            idx = lambda i, nd=a.ndim: (0,) * (nd - 2) + (i, 0)
            ins.append(pl.BlockSpec(blk, idx))
            outs.append(pl.BlockSpec(blk, idx))
            shapes.append(jax.ShapeDtypeStruct(a.shape, BF16))
        arrays.append(a)
    return arrays, ins, outs, shapes


def _dot(a, b):
    return jnp.dot(a, b, preferred_element_type=F32)


def _rdot(a, b):
    n = a.shape[0] // DOT_ROWS
    return jnp.concatenate([_dot(a[r * DOT_ROWS:(r + 1) * DOT_ROWS], b) for r in range(n)], axis=0)


def _prologue_kernel(x_ref, g_ref, pos_ref, fm_ref, fr_ref, h_out, o_ref):
    h_out[...] = _normed_bf16(x_ref[...], g_ref)
    pos = pos_ref[...]
    lane = lax.broadcasted_iota(jnp.int32, (LANES, pos.shape[1]), 0)
    for n, (f_ref, is_x1, is_x2) in enumerate((
            (fm_ref, lane < MLA_ROPE // 2, (lane >= MLA_ROPE // 2) & (lane < MLA_ROPE)),
            (fr_ref, (lane % RET_DK) < RET_DK // 2, (lane % RET_DK) >= RET_DK // 2))):
        ang = f_ref[...] * pos
        reps = LANES // ang.shape[0]
        cos_t = jnp.concatenate([jnp.cos(ang)] * reps, axis=0)
        sin_t = jnp.concatenate([jnp.sin(ang)] * reps, axis=0)
        o_ref[:, (3 * n) * LANES:(3 * n + 1) * LANES] = cos_t.T
        o_ref[:, (3 * n + 1) * LANES:(3 * n + 2) * LANES] = jnp.where(is_x1, -sin_t, 0.0).T
        o_ref[:, (3 * n + 2) * LANES:(3 * n + 3) * LANES] = jnp.where(is_x2, sin_t, 0.0).T


def _prologue(x, g_mix, posf, mla_freq, ret_freq):
    tm = ROW_TILE
    row = pl.BlockSpec((tm, D_MODEL), lambda i: (i, 0))
    return pl.pallas_call(
        _prologue_kernel,
        grid=(SEQ // tm,),
        in_specs=[row, _lay(g_mix, 0), pl.BlockSpec((1, tm), lambda i: (0, i)), _const(mla_freq),
                  _const(ret_freq)],
        out_specs=[row, pl.BlockSpec((tm, 6 * LANES), lambda i: (i, 0))],
        out_shape=[jax.ShapeDtypeStruct((SEQ, D_MODEL), BF16), jax.ShapeDtypeStruct((SEQ, 6 * LANES), F32)],
        compiler_params=_params(("parallel",)),
        name="prologue",
    )(x, g_mix, posf, mla_freq, ret_freq)


def _rope_spec(which):
    return pl.BlockSpec((ROW_TILE, 3 * LANES), lambda i: (i, which))


def _rope_slab(x, cos_t, sin_a, sin_b):
    return (x * cos_t + pltpu.roll(x, LANES - 32, axis=1) * sin_a
            + pltpu.roll(x, 32, axis=1) * sin_b)


def _mla_pre_kernel(h_ref, rope_ref, wa_ref, gqa_ref, gkva_ref, wq_ref, wkv_ref,
                    gq_ref, gkn_ref, gkr_ref, qt_out, k_out, vt_out):
    z = _rdot(h_ref[...], wa_ref[...])
    zq = z[:, :Q_LORA]
    zkv = z[:, Q_LORA:Q_LORA + KV_LORA]
    lane = lax.broadcasted_iota(jnp.int32, (z.shape[0], LANES), 1)
    zkr = jnp.where(lane < MLA_ROPE, z[:, Q_LORA + KV_LORA:], 0.0)
    c_q = (zq * _rms_scale(zq, Q_LORA) * gqa_ref[...]).astype(BF16)
    c_kv = (zkv * _rms_scale(zkv, KV_LORA) * gkva_ref[...]).astype(BF16)

    cos_t, sin_a, sin_b = (rope_ref[:, n * LANES:(n + 1) * LANES] for n in range(3))

    q = _rdot(c_q, wq_ref[...])
    kv = _rdot(c_kv, wkv_ref[...])

    kr = zkr * gkr_ref[...]
    ss_kr = jnp.sum(zkr * zkr, axis=-1, keepdims=True)
    kr_rot = _rope_slab(kr, cos_t, sin_a, sin_b)
    ones = jnp.ones((V_ONES, z.shape[0]), BF16)
    for hd in range(MLA_HEADS):
        qb = q[:, hd * HEAD_PAD:(hd + 1) * HEAD_PAD]
        qn = qb * _rms_scale(qb, MLA_QK) * gq_ref[...]
        qt_out[hd, 0:LANES, :] = qn[:, :LANES].T.astype(BF16)
        qt_out[hd, LANES:, :] = _rope_slab(qn[:, LANES:], cos_t, sin_a, sin_b).T.astype(BF16)
        vh = kv[:, MLA_HEADS * MLA_NOPE + hd * MLA_V:MLA_HEADS * MLA_NOPE + (hd + 1) * MLA_V]
        vt_out[hd, 0, 0:MLA_V, :] = vh.T.astype(BF16)
        vt_out[hd, 0, MLA_V:, :] = ones
        kn = kv[:, hd * MLA_NOPE:(hd + 1) * MLA_NOPE]
        r_k = lax.rsqrt((jnp.sum(kn * kn, axis=-1, keepdims=True) + ss_kr) * (1.0 / MLA_QK) + EPS)
        k_out[:, hd * HEAD_PAD:hd * HEAD_PAD + LANES] = (kn * r_k * gkn_ref[...]).astype(BF16)
        k_out[:, hd * HEAD_PAD + LANES:(hd + 1) * HEAD_PAD] = (kr_rot * r_k).astype(BF16)


def _mla_pre(l, h, rope, wa, g_qa, g_kva, wq, wkv, gq, gkn, gkr):
    tm = ROW_TILE
    row = lambda w: pl.BlockSpec((tm, w), lambda i: (i, 0))
    return pl.pallas_call(
        _mla_pre_kernel,
        grid=(SEQ // tm,),
        in_specs=[row(D_MODEL), _rope_spec(0)]
        + [_lay(a, l) for a in (wa, g_qa, g_kva, wq, wkv, gq, gkn, gkr)],
        out_specs=[pl.BlockSpec((MLA_HEADS, HEAD_PAD, tm), lambda i: (0, 0, i)),
                   row(MLA_HEADS * HEAD_PAD),
                   pl.BlockSpec((MLA_HEADS, 1, MLA_V + V_ONES, tm), lambda i: (0, i, 0, 0))],
        out_shape=[jax.ShapeDtypeStruct((MLA_HEADS, HEAD_PAD, SEQ), BF16),
                   jax.ShapeDtypeStruct((SEQ, MLA_HEADS * HEAD_PAD), BF16),
                   jax.ShapeDtypeStruct((MLA_HEADS, SEQ // tm, MLA_V + V_ONES, tm), BF16)],
        compiler_params=_params(("parallel",)),
        name="mla_pre",
    )(h, rope, wa, g_qa, g_kva, wq, wkv, gq, gkn, gkr)


def _flash_kernel(qt_ref, k_ref, vt_ref, o_ref, m_scr, acc_scr, st_a, st_b):
    t = ATT_TILE
    i = pl.program_id(1)
    sub = SUBLANES
    pack = 2 * SUBLANES
    vrows = MLA_V + V_ONES
    m_scr[...] = jnp.full(m_scr.shape, MASK_VALUE, F32)
    acc_scr[...] = jnp.zeros(acc_scr.shape, F32)

    def logits(j, dst):
        start = pl.multiple_of(j * t, t)
        for hh in range(ATT_HEADS_PER_STEP):
            dst[hh] = _dot(k_ref[pl.ds(start, t), hh * HEAD_PAD:(hh + 1) * HEAD_PAD], qt_ref[hh])

    def softmax_pv(src, j, masked):
        if masked:
            key = lax.broadcasted_iota(jnp.int32, (t, t), 0)
            qry = lax.broadcasted_iota(jnp.int32, (t, t), 1)
            keep = key <= qry
        for hh in range(ATT_HEADS_PER_STEP):
            st = src[hh]
            if masked:
                st = jnp.where(keep, st, MASK_VALUE)
            s3 = st.reshape(t // pack, pack, t)
            mx = jnp.max(s3, axis=0)
            mx = jnp.maximum(mx[:sub], mx[sub:])
            for shift in (4, 2, 1):
                mx = jnp.maximum(mx, pltpu.roll(mx, shift, axis=0))
            m_prev = m_scr[hh]
            m_new = jnp.maximum(m_prev, mx)
            alpha = jnp.exp2(m_prev - m_new)
            m2 = jnp.concatenate([m_new, m_new], axis=0)
            p = jnp.exp2((s3 - m2[None]).astype(BF16)).reshape(t, t)
            pv = _dot(vt_ref[hh, j], p)
            acc = acc_scr[hh].reshape(vrows // sub, sub, t) * alpha[None] + pv.reshape(vrows // sub, sub, t)
            acc_scr[hh] = acc.reshape(vrows, t)
            m_scr[hh] = m_new

    logits(0, st_a)

    def pair(m, carry):
        logits(2 * m + 1, st_b)
        softmax_pv(st_a, 2 * m, False)
        logits(2 * m + 2, st_a)
        softmax_pv(st_b, 2 * m + 1, False)
        return carry

    lax.fori_loop(0, lax.shift_right_logical(i, 1), pair, 0)
    odd = (i & 1) == 1

    @pl.when(odd)
    def _():
        logits(i, st_b)
        softmax_pv(st_a, i - 1, False)
        softmax_pv(st_b, i, True)

    @pl.when(jnp.logical_not(odd))
    def _():
        softmax_pv(st_a, i, True)

    for hh in range(ATT_HEADS_PER_STEP):
        num = acc_scr[hh, 0:MLA_V, :].reshape(MLA_V // sub, sub, t)
        den = acc_scr[hh, MLA_V:MLA_V + sub, :]
        out_t = (num / den[None]).reshape(MLA_V, t)
        o_ref[:, hh * MLA_V:(hh + 1) * MLA_V] = out_t.T.astype(o_ref.dtype)


def _flash(qt, k, vt):
    t, hp = ATT_TILE, ATT_HEADS_PER_STEP
    vrows = MLA_V + V_ONES
    return pl.pallas_call(
        _flash_kernel,
        grid=(MLA_HEADS // hp, SEQ // t),
        in_specs=[pl.BlockSpec((hp, HEAD_PAD, t), lambda h, i: (h, 0, i)),
                  pl.BlockSpec((SEQ, hp * HEAD_PAD), lambda h, i: (0, h), pipeline_mode=pl.Buffered(1)),
                  pl.BlockSpec((hp, SEQ // t, vrows, t), lambda h, i: (h, 0, 0, 0),
                               pipeline_mode=pl.Buffered(1))],
        out_specs=pl.BlockSpec((t, hp * MLA_V), lambda h, i: (i, h)),
        out_shape=jax.ShapeDtypeStruct((SEQ, MLA_HEADS * MLA_V), BF16),
        scratch_shapes=[pltpu.VMEM((hp, SUBLANES, t), F32), pltpu.VMEM((hp, vrows, t), F32),
                        pltpu.VMEM((hp, t, t), F32), pltpu.VMEM((hp, t, t), F32)],
        compiler_params=_params(("parallel", "arbitrary")),
        name="flash",
    )(qt, k, vt)


def _sconv_kernel(h_ref, w_ref, wsc_ref, o_ref, buf):
    tm = ROW_TILE
    z = _rdot(h_ref[...], w_ref[...])

    @pl.when(pl.program_id(0) == 0)
    def _():
        buf[0:SUBLANES, :] = jnp.zeros((SUBLANES, GROUP), F32)

    g = z[:, GROUP:2 * GROUP] * z[:, 2 * GROUP:]
    buf[SUBLANES:, :] = g
    w = wsc_ref[...]
    conv = (w[2:3, :] * g + w[1:2, :] * buf[SUBLANES - 1:SUBLANES - 1 + tm, :]
            + w[0:1, :] * buf[SUBLANES - 2:SUBLANES - 2 + tm, :])
    o_ref[...] = (z[:, :GROUP] * conv).astype(o_ref.dtype)
    buf[0:SUBLANES, :] = buf[tm:tm + SUBLANES, :]


def _sconv(l, h, w_in, wsc, cast=()):
    tm = ROW_TILE
    n = SEQ // tm
    c_arr, c_in, c_out, c_shape = _cast_specs(cast, n)
    return pl.pallas_call(
        _with_casts(_sconv_kernel, 3, 1, len(cast)),
        grid=(n,),
        in_specs=[pl.BlockSpec((tm, D_MODEL), lambda i: (i, 0)),
                  _w_in_spec(l, "sconv"), _lay(wsc, l)] + c_in,
        out_specs=[pl.BlockSpec((tm, GROUP), lambda i: (i, 0))] + c_out,
        out_shape=[jax.ShapeDtypeStruct((SEQ, GROUP), BF16)] + c_shape,
        scratch_shapes=[pltpu.VMEM((tm + SUBLANES, GROUP), F32)],
        compiler_params=_params(("arbitrary",)),
        name="sconv",
    )(h, w_in, wsc, *c_arr)


def _conformer_kernel(h_ref, w_ref, wcf_ref, bcf_ref, gln_ref, bln_ref, o_ref, buf, wbuf, uscr):
    tm = ROW_TILE
    nslab = GROUP // LANES
    z = _rdot(h_ref[...], w_ref[...])

    @pl.when(pl.program_id(0) == 0)
    def _():
        buf[:, 0:CF_HALO, :] = jnp.zeros((nslab, CF_HALO, LANES), F32)

    glu = z[:, :GROUP] * jax.nn.sigmoid(z[:, GROUP:])
    for sl in range(nslab):
        buf[sl, CF_HALO:, :] = glu[:, sl * LANES:(sl + 1) * LANES]
    base = CF_HALO - (CF_K - 1)
    for phase in range(1, SUBLANES):
        wbuf[phase - 1] = buf[:, phase:phase + tm + CF_HALO - SUBLANES, :]

    def tap_rows(o, r0, sl):
        phase = o % SUBLANES
        start = pl.multiple_of(r0 + (o - phase), SUBLANES)
        if phase == 0:
            return buf[sl, pl.ds(start, CF_ROWS), :]
        return wbuf[phase - 1, sl, pl.ds(start, CF_ROWS), :]

    def row_block(rb, carry):
        r0 = pl.multiple_of(rb * CF_ROWS, CF_ROWS)
        for sl in range(nslab):
            lanes = slice(sl * LANES, (sl + 1) * LANES)
            acc = jnp.broadcast_to(bcf_ref[:, lanes], (CF_ROWS, LANES))
            for j in range(CF_K):
                acc = acc + wcf_ref[j:j + 1, lanes] * tap_rows(base + j, r0, sl)
            uscr[pl.ds(r0, CF_ROWS), lanes] = acc
        return carry

    lax.fori_loop(0, tm // CF_ROWS, row_block, 0)
    u = uscr[...]
    mu = jnp.mean(u, axis=-1, keepdims=True)
    uc = u - mu
    y = uc * lax.rsqrt(jnp.mean(uc * uc, axis=-1, keepdims=True) + EPS)
    y = y * gln_ref[...] + bln_ref[...]
    o_ref[...] = (y * jax.nn.sigmoid(y)).astype(o_ref.dtype)
    buf[:, 0:CF_HALO, :] = buf[:, tm:tm + CF_HALO, :]


def _conformer(l, h, w_in, wcf, bcf, gln, bln, cast=()):
    tm = ROW_TILE
    n = SEQ // tm
    c_arr, c_in, c_out, c_shape = _cast_specs(cast, n)
    return pl.pallas_call(
        _with_casts(_conformer_kernel, 6, 1, len(cast)),
        grid=(n,),
        in_specs=[pl.BlockSpec((tm, D_MODEL), lambda i: (i, 0)), _w_in_spec(l, "conformer")]
        + [_lay(a, l) for a in (wcf, bcf, gln, bln)] + c_in,
        out_specs=[pl.BlockSpec((tm, GROUP), lambda i: (i, 0))] + c_out,
        out_shape=[jax.ShapeDtypeStruct((SEQ, GROUP), BF16)] + c_shape,
        scratch_shapes=[pltpu.VMEM((GROUP // LANES, tm + CF_HALO, LANES), F32),
                        pltpu.VMEM((SUBLANES - 1, GROUP // LANES, tm + CF_HALO - SUBLANES, LANES), F32),
                        pltpu.VMEM((tm, GROUP), F32)],
        compiler_params=_params(("arbitrary",)),
        name="conformer",
    )(h, w_in, wcf, bcf, gln, bln, *c_arr)


def _retention_kernel(h_ref, rope_ref, wqk_ref, wv_ref, wg_ref, qmask_ref, kdec_ref, qdec_ref,
                      inner_ref, cdec_ref, gret_ref, o_ref, state):
    tm = ROW_TILE
    c = RET_CHUNK
    dk2 = RET_HEADS * RET_DK

    @pl.when(pl.program_id(0) == 0)
    def _():
        state[...] = jnp.zeros(state.shape, F32)

    half = RET_DK // 2

    for ci in range(tm // c):
        rows = slice(ci * c, (ci + 1) * c)
        hb = h_ref[rows, :]
        zqk = _dot(hb, wqk_ref[...])
        v = _dot(hb, wv_ref[...])
        gate = _dot(hb, wg_ref[...])
        cos_t, sin_a, sin_b = (rope_ref[rows, n * LANES:(n + 1) * LANES] for n in range(3))

        def rot(t):
            return jnp.concatenate(
                [x * cos_t + pltpu.roll(x, LANES - half, axis=1) * sin_a + pltpu.roll(x, half, axis=1) * sin_b
                 for x in (t[:, :LANES], t[:, LANES:])], axis=1)

        qc = rot(zqk[:, :dk2])
        kt = (rot(zqk[:, dk2:]) * (RET_DK ** -0.5)).T
        kt_b = kt.astype(BF16)
        for hd in range(RET_HEADS):
            cols = slice(hd * RET_DV, (hd + 1) * RET_DV)
            qm = (qc * qmask_ref[hd]).astype(BF16)
            vh = v[:, cols].astype(BF16)
            sc = _dot(qm, kt_b) * inner_ref[hd]
            st = state[hd]
            o = _dot(sc.astype(BF16), vh) + _dot(qm, st.astype(BF16)) * qdec_ref[hd]
            state[hd] = st * cdec_ref[hd] + _dot((kt * kdec_ref[hd]).astype(BF16), vh)
            on = o * _rms_scale(o, RET_DV) * gret_ref[hd]
            gh = gate[:, cols]
            o_ref[rows, cols] = (gh * jax.nn.sigmoid(gh) * on).astype(o_ref.dtype)


def _retention(l, h, rope, w_in, qmask, kdec, qdec, inner, cdec, gret, cast=()):
    tm = ROW_TILE
    n = SEQ // tm
    c_arr, c_in, c_out, c_shape = _cast_specs(cast, n)
    return pl.pallas_call(
        _with_casts(_retention_kernel, 11, 1, len(cast)),
        grid=(n,),
        in_specs=[pl.BlockSpec((tm, D_MODEL), lambda i: (i, 0)), _rope_spec(1), _w_in_spec(l, "ret_qk"),
                  _w_in_spec(l, "ret_v"), _w_in_spec(l, "ret_gate")]
        + [_const(a) for a in (qmask, kdec, qdec, inner, cdec)] + [_lay(gret, l)] + c_in,
        out_specs=[pl.BlockSpec((tm, GROUP), lambda i: (i, 0))] + c_out,
        out_shape=[jax.ShapeDtypeStruct((SEQ, GROUP), BF16)] + c_shape,
        scratch_shapes=[pltpu.VMEM((RET_HEADS, RET_HEADS * RET_DK, RET_DV), F32)],
        compiler_params=_params(("arbitrary",)),
        name="retention",
    )(h, rope, w_in, w_in, w_in, qmask, kdec, qdec, inner, cdec, gret, *c_arr)


def _out_proj_kernel(x_ref, ya_ref, yb_ref, yc_ref, yd_ref, wo_ref, g_ref, o_ref, h_out):
    acc = x_ref[...]
    for n, y_ref in enumerate((ya_ref, yb_ref, yc_ref, yd_ref)):
        acc = acc + _dot(y_ref[...], wo_ref[n * GROUP:(n + 1) * GROUP, :])
    o_ref[...] = acc
    h_out[...] = _normed_bf16(acc, g_ref)


def _out_proj(l, x, ya, yb, yc, yd, wo, g_ffn):
    tm = ROW_TILE
    row = lambda n: pl.BlockSpec((tm, n), lambda i: (i, 0))
    return pl.pallas_call(
        _out_proj_kernel,
        grid=(SEQ // tm,),
        in_specs=[row(D_MODEL), row(GROUP), row(GROUP), row(GROUP), row(GROUP), _lay(wo, l),
                  _lay(g_ffn, l)],
        out_specs=[row(D_MODEL), row(D_MODEL)],
        out_shape=[jax.ShapeDtypeStruct((SEQ, D_MODEL), F32), jax.ShapeDtypeStruct((SEQ, D_MODEL), BF16)],
        compiler_params=_params(("parallel",)),
        name="out_proj",
    )(x, ya, yb, yc, yd, wo, g_ffn)


def _ffn_kernel(x_ref, h_ref, wg_ref, wu_ref, cg_ref, cu_ref, wd_ref, o_ref, ubuf, carry):
    tm = ROW_TILE
    i = pl.program_id(0)
    j = pl.program_id(1)

    @pl.when(j == 0)
    def _():
        o_ref[...] = x_ref[...]

    @pl.when(i == 0)
    def _():
        carry[j] = jnp.zeros(carry.shape[1:], F32)

    ts = FF_TILE // FF_SUB
    h = h_ref[...]

    def conv(part, s, u, c_ref):
        cols = slice(s * ts, (s + 1) * ts)
        ubuf[part, s, 0:SUBLANES, :] = carry[j, part, :, cols]
        ubuf[part, s, SUBLANES:, :] = u
        carry[j, part, :, cols] = u[tm - SUBLANES:, :]
        w = c_ref[:, cols]
        return (w[2:3, :] * u + w[1:2, :] * ubuf[part, s, SUBLANES - 1:SUBLANES - 1 + tm, :]
                + w[0:1, :] * ubuf[part, s, SUBLANES - 2:SUBLANES - 2 + tm, :])

    us = [(_rdot(h, wg_ref[:, s * ts:(s + 1) * ts]), _rdot(h, wu_ref[:, s * ts:(s + 1) * ts]))
          for s in range(FF_SUB)]
    for s, (ug, uu) in enumerate(us):
        gate = conv(0, s, ug, cg_ref)
        up = conv(1, s, uu, cu_ref)
        act = (gate * jax.nn.sigmoid(gate) * up).astype(BF16)
        o_ref[...] += _dot(act, wd_ref[s * ts:(s + 1) * ts, :])


def _ffn(l, x, h, wg, wu, wd, w_conv):
    tm, tf = ROW_TILE, FF_TILE
    nf = D_FF // tf
    return pl.pallas_call(
        _ffn_kernel,
        grid=(SEQ // tm, nf),
        in_specs=[pl.BlockSpec((tm, D_MODEL), lambda i, j: (i, 0)),
                  pl.BlockSpec((tm, D_MODEL), lambda i, j: (i, 0)),
                  pl.BlockSpec((D_MODEL, tf), lambda i, j: (0, j)),
                  pl.BlockSpec((D_MODEL, tf), lambda i, j: (0, j)),
                  pl.BlockSpec((None, FFN_K, tf), lambda i, j: (l, 0, j)),
                  pl.BlockSpec((None, FFN_K, tf), lambda i, j: (l, 0, j + nf)),
                  pl.BlockSpec((tf, D_MODEL), lambda i, j: (j, 0))],
        out_specs=pl.BlockSpec((tm, D_MODEL), lambda i, j: (i, 0)),
        out_shape=jax.ShapeDtypeStruct((SEQ, D_MODEL), F32),
        scratch_shapes=[pltpu.VMEM((2, FF_SUB, tm + SUBLANES, tf // FF_SUB), F32),
                        pltpu.VMEM((nf, 2, SUBLANES, tf), F32)],
        compiler_params=_params(("arbitrary", "arbitrary")),
        name="ffn",
    )(x, h, wg, wu, w_conv, w_conv, wd)


def _ple_kernel(x_ref, p_ref, g_ref, wpg_ref, wpe_ref, *rest):
    o_ref = rest[-2] if len(rest) == 3 else rest[-1]
    hn = _normed_bf16(x_ref[...], g_ref)
    pb = p_ref[...].astype(BF16)
    for n in range(D_MODEL // GROUP):
        cols = slice(n * GROUP, (n + 1) * GROUP)
        gate = jax.nn.sigmoid(_dot(hn, wpg_ref[:, cols]))
        o_ref[:, cols] = x_ref[:, cols] + _dot(pb, wpe_ref[:, cols]) * gate
    if len(rest) == 3:
        gnext_ref, _, h_out = rest
        h_out[...] = _normed_bf16(o_ref[...], gnext_ref)


def _ple(l, x, p, g, wpg, wpe, g_mix):
    tm = ROW_TILE
    row = lambda n: pl.BlockSpec((tm, n), lambda i: (i, 0))
    last = l == DEPTH - 1
    in_specs = [row(D_MODEL), pl.BlockSpec((None, None, tm, PLE_DIM), lambda i: (l, 0, i, 0)),
                _lay(g, l), _lay(wpg, l), _lay(wpe, l)]
    out_specs = [row(D_MODEL)]
    out_shape = [jax.ShapeDtypeStruct((SEQ, D_MODEL), F32)]
    args = [x, p, g, wpg, wpe]
    if not last:
        in_specs.append(_lay(g_mix, l + 1))
        out_specs.append(row(D_MODEL))
        out_shape.append(jax.ShapeDtypeStruct((SEQ, D_MODEL), BF16))
        args.append(g_mix)
    outs = pl.pallas_call(
        _ple_kernel,
        grid=(SEQ // tm,),
        in_specs=in_specs,
        out_specs=out_specs,
        out_shape=out_shape,
        compiler_params=_params(("parallel",)),
        name="ple",
    )(*args)
    return (outs[0], None) if last else (outs[0], outs[1])


def _retention_tables():
    c = RET_CHUNK
    lg = jnp.log(1.0 - 2.0 ** (-5.0 - jnp.arange(RET_HEADS, dtype=F32)))
    idx = jnp.arange(c, dtype=F32)
    rel = idx[:, None] - idx[None, :]
    inner = jnp.where(rel[None] >= 0, jnp.exp(lg[:, None, None] * rel[None]), 0.0)
    qdec = jnp.exp(lg[:, None] * (idx + 1.0))[:, :, None]
    kdec_t = jnp.exp(lg[:, None] * (c - 1.0 - idx))
    cdec = jnp.exp(lg * c)
    lane = np.arange(RET_HEADS * RET_DK)
    head_of_lane = lane // RET_DK
    hmask = jnp.asarray(head_of_lane[None, :] == np.arange(RET_HEADS)[:, None], F32)
    qmask = hmask[:, None, :]
    kdec = hmask[:, :, None] * kdec_t[:, None, :]
    cdec_b = jnp.broadcast_to(cdec[:, None, None], (RET_HEADS, 1, RET_DV))
    return qmask, kdec, qdec, inner, cdec_b


def kernel(x, p, positions, g_mix, w_in, g_qa, g_kva, w_q_up, w_kv_up, g_qn, g_kn, w_sc, w_cf,
           b_cf, g_cf_ln, b_cf_ln, g_ret, w_o, g_ffn, w_up, w_ffn_conv, w_down, g_pe, w_pe, w_pg):
    assert x.shape == (1, SEQ, D_MODEL)
    xs = x[0]
    posf = positions.astype(F32)
    rows = lambda a: a[:, None, :]

    mla_freq = (ROPE_BASE ** (-jnp.arange(0, MLA_ROPE, 2, dtype=F32) / MLA_ROPE))[:, None]
    ret_freq = (1.0 / (10000.0 ** jnp.linspace(0.0, 1.0, RET_DK // 2, dtype=F32)))[:, None]
    qmask, kdec, qdec, inner, cdec = _retention_tables()

    w_in_r = jnp.pad(w_in.astype(BF16), ((0, 0), (0, 0), (W_IN_SHIFT, 0)))
    wa = w_in[:, :, :MLA_IN].astype(BF16)
    wq = jnp.pad(w_q_up.reshape(DEPTH, Q_LORA, MLA_HEADS, MLA_QK),
                 ((0, 0), (0, 0), (0, 0), (0, HEAD_PAD - MLA_QK))
                 ).reshape(DEPTH, Q_LORA, MLA_HEADS * HEAD_PAD).astype(BF16)
    wkv4 = w_kv_up.reshape(DEPTH, KV_LORA, MLA_HEADS, MLA_NOPE + MLA_V)
    wkv = jnp.concatenate([wkv4[..., :MLA_NOPE].reshape(DEPTH, KV_LORA, -1),
                           wkv4[..., MLA_NOPE:].reshape(DEPTH, KV_LORA, -1)], axis=2).astype(BF16)
    gq = jnp.pad(g_qn * (MLA_QK ** -0.5 * LOG2_E), ((0, 0), (0, HEAD_PAD - MLA_QK)))
    gkr = jnp.pad(g_kn[:, MLA_NOPE:], ((0, 0), (0, LANES - MLA_ROPE)))
    gm, g_ret4 = rows(g_mix), g_ret[:, :, None, :]

    hm, rope = _prologue(xs, gm, posf, mla_freq, ret_freq)
    for l in range(DEPTH):
        qt, k, vt = _mla_pre(l, hm, rope, wa, rows(g_qa), rows(g_kva), wq, wkv, rows(gq),
                             rows(g_kn[:, :MLA_NOPE]), rows(gkr))
        y_a = _flash(qt, k, vt)
        first = l == 0
        y_b, wd, *cast_b = _sconv(l, hm, w_in_r, w_sc,
                                  cast=((w_down, l, 1, 0),) + ((w_o,) if first else ()))
        y_c, *cast_c = _conformer(l, hm, w_in_r, w_cf, rows(b_cf), rows(g_cf_ln), rows(b_cf_ln),
                                  cast=(w_pg, w_pe) if first else ())
        if first:
            (w_o_b,), (w_pg_b, w_pe_b) = cast_b, cast_c
        y_d, wg, wu = _retention(l, hm, rope, w_in_r, qmask, kdec, qdec, inner, cdec, g_ret4,
                                 cast=((w_up, l, 2, 0), (w_up, l, 2, 1)))
        xs, hf = _out_proj(l, xs, y_a, y_b, y_c, y_d, w_o_b, rows(g_ffn))
        xs = _ffn(l, xs, hf, wg, wu, wd, w_ffn_conv)
        xs, hm = _ple(l, xs, p, rows(g_pe), w_pg_b, w_pe_b, gm)
    return xs[None]
```

```python
import numpy as np
import jax
import jax.numpy as jnp
from jax import lax
from jax.experimental import pallas as pl
from jax.experimental.pallas import tpu as pltpu

F32 = jnp.float32
BF16 = jnp.bfloat16

D_MODEL = 2048
SEQ = 8192
DEPTH = 2
GROUP = 512
MLA_HEADS = 4
MLA_NOPE = 128
MLA_ROPE = 64
MLA_V = 128
MLA_QK = MLA_NOPE + MLA_ROPE
Q_LORA = 512
KV_LORA = 256
ROPE_BASE = 10000.0
SC_K = 3
CF_K = 31
RET_HEADS = 4
RET_DV = 128
RET_DK = 64
RET_CHUNK = 128
D_FF = 5632
FFN_K = 3
PLE_DIM = 256
EPS = 1e-6
N_IN = Q_LORA + KV_LORA + MLA_ROPE + 3 * GROUP + 2 * GROUP + 2 * RET_HEADS * (RET_DK + RET_DV)

LANES = 128
SUBLANES = 8
HEAD_PAD = 2 * LANES
ROW_TILE = 512
FF_TILE = 512
FF_SUB = 2
DOT_ROWS = 128
ATT_TILE = 512
ATT_HEADS_PER_STEP = 4
V_ONES = 16
CF_HALO = 32
CF_ROWS = 128
MLA_IN = Q_LORA + KV_LORA + LANES
W_IN_SHIFT = 3 * GROUP - (Q_LORA + KV_LORA + MLA_ROPE)
W_IN_GROUPS = {"sconv": (3 * GROUP, 1), "conformer": (2 * GROUP, 3), "ret_qk": (GROUP, 8),
               "ret_v": (GROUP, 9), "ret_gate": (GROUP, 10)}
VMEM_LIMIT = 56 * 1024 * 1024
MASK_VALUE = -1e30
LOG2_E = 1.4426950408889634


def _lay(a, l):
    return pl.BlockSpec((None,) + a.shape[1:], lambda *_: (l,) + (0,) * (a.ndim - 1))


def _const(a):
    return pl.BlockSpec(a.shape, lambda *_: (0,) * a.ndim)


def _w_in_spec(l, group):
    width, block = W_IN_GROUPS[group]
    return pl.BlockSpec((None, D_MODEL, width), lambda *_: (l, 0, block))


def _params(sem):
    return pltpu.CompilerParams(dimension_semantics=sem, vmem_limit_bytes=VMEM_LIMIT)


def _rms_scale(xf, n):
    return lax.rsqrt(jnp.sum(xf * xf, axis=-1, keepdims=True) * (1.0 / n) + EPS)


def _normed_bf16(xf, g_ref):
    return (xf * _rms_scale(xf, D_MODEL) * g_ref[...]).astype(BF16)


def _with_casts(body, n_in, n_out, n_cast):
    def kernel_fn(*refs):
        ins, srcs = refs[:n_in], refs[n_in:n_in + n_cast]
        outs = refs[n_in + n_cast:n_in + n_cast + n_out]
        dsts = refs[n_in + n_cast + n_out:n_in + 2 * n_cast + n_out]
        for src, dst in zip(srcs, dsts):
            dst[...] = src[...].astype(BF16)
        body(*ins, *outs, *refs[n_in + 2 * n_cast + n_out:])
    return kernel_fn


def _cast_specs(items, steps):
    arrays, ins, outs, shapes = [], [], [], []
    for item in items:
        if isinstance(item, tuple):
            a, layer, ncol, col = item
            rows, cols = a.shape[-2] // steps, a.shape[-1] // ncol
            ins.append(pl.BlockSpec((None, rows, cols), lambda i, layer=layer, col=col: (layer, i, col)))
            outs.append(pl.BlockSpec((rows, cols), lambda i: (i, 0)))
            shapes.append(jax.ShapeDtypeStruct((a.shape[-2], cols), BF16))
        else:
            a = item
            blk = a.shape[:-2] + (a.shape[-2] // steps, a.shape[-1])
            idx = lambda i, nd=a.ndim: (0,) * (nd - 2) + (i, 0)
            ins.append(pl.BlockSpec(blk, idx))
            outs.append(pl.BlockSpec(blk, idx))
            shapes.append(jax.ShapeDtypeStruct(a.shape, BF16))
        arrays.append(a)
    return arrays, ins, outs, shapes


def _dot(a, b):
    return jnp.dot(a, b, preferred_element_type=F32)


def _rdot(a, b):
    n = a.shape[0] // DOT_ROWS
    return jnp.concatenate([_dot(a[r * DOT_ROWS:(r + 1) * DOT_ROWS], b) for r in range(n)], axis=0)


def _prologue_kernel(x_ref, g_ref, pos_ref, fm_ref, fr_ref, h_out, o_ref):
    h_out[...] = _normed_bf16(x_ref[...], g_ref)
    pos = pos_ref[...]
    lane = lax.broadcasted_iota(jnp.int32, (LANES, pos.shape[1]), 0)
    for n, (f_ref, is_x1, is_x2) in enumerate((
            (fm_ref, lane < MLA_ROPE // 2, (lane >= MLA_ROPE // 2) & (lane < MLA_ROPE)),
            (fr_ref, (lane % RET_DK) < RET_DK // 2, (lane % RET_DK) >= RET_DK // 2))):
        ang = f_ref[...] * pos
        reps = LANES // ang.shape[0]
        cos_t = jnp.concatenate([jnp.cos(ang)] * reps, axis=0)
        sin_t = jnp.concatenate([jnp.sin(ang)] * reps, axis=0)
        o_ref[:, (3 * n) * LANES:(3 * n + 1) * LANES] = cos_t.T
        o_ref[:, (3 * n + 1) * LANES:(3 * n + 2) * LANES] = jnp.where(is_x1, -sin_t, 0.0).T
        o_ref[:, (3 * n + 2) * LANES:(3 * n + 3) * LANES] = jnp.where(is_x2, sin_t, 0.0).T


def _prologue(x, g_mix, posf, mla_freq, ret_freq):
    tm = ROW_TILE
    row = pl.BlockSpec((tm, D_MODEL), lambda i: (i, 0))
    return pl.pallas_call(
        _prologue_kernel,
        grid=(SEQ // tm,),
        in_specs=[row, _lay(g_mix, 0), pl.BlockSpec((1, tm), lambda i: (0, i)), _const(mla_freq),
                  _const(ret_freq)],
        out_specs=[row, pl.BlockSpec((tm, 6 * LANES), lambda i: (i, 0))],
        out_shape=[jax.ShapeDtypeStruct((SEQ, D_MODEL), BF16), jax.ShapeDtypeStruct((SEQ, 6 * LANES), F32)],
        compiler_params=_params(("parallel",)),
        name="prologue",
    )(x, g_mix, posf, mla_freq, ret_freq)


def _rope_spec(which):
    return pl.BlockSpec((ROW_TILE, 3 * LANES), lambda i: (i, which))


def _rope_slab(x, cos_t, sin_a, sin_b):
    return (x * cos_t + pltpu.roll(x, LANES - 32, axis=1) * sin_a
            + pltpu.roll(x, 32, axis=1) * sin_b)


def _mla_pre_kernel(h_ref, rope_ref, wa_ref, gqa_ref, gkva_ref, wq_ref, wkv_ref,
                    gq_ref, gkn_ref, gkr_ref, qt_out, k_out, vt_out):
    z = _rdot(h_ref[...], wa_ref[...])
    zq = z[:, :Q_LORA]
    zkv = z[:, Q_LORA:Q_LORA + KV_LORA]
    lane = lax.broadcasted_iota(jnp.int32, (z.shape[0], LANES), 1)
    zkr = jnp.where(lane < MLA_ROPE, z[:, Q_LORA + KV_LORA:], 0.0)
    c_q = (zq * _rms_scale(zq, Q_LORA) * gqa_ref[...]).astype(BF16)
    c_kv = (zkv * _rms_scale(zkv, KV_LORA) * gkva_ref[...]).astype(BF16)

    cos_t, sin_a, sin_b = (rope_ref[:, n * LANES:(n + 1) * LANES] for n in range(3))

    q = _rdot(c_q, wq_ref[...])
    kv = _rdot(c_kv, wkv_ref[...])

    kr = zkr * gkr_ref[...]
    ss_kr = jnp.sum(zkr * zkr, axis=-1, keepdims=True)
    kr_rot = _rope_slab(kr, cos_t, sin_a, sin_b)
    ones = jnp.ones((V_ONES, z.shape[0]), BF16)
    for hd in range(MLA_HEADS):
        qb = q[:, hd * HEAD_PAD:(hd + 1) * HEAD_PAD]
        qn = qb * _rms_scale(qb, MLA_QK) * gq_ref[...]
        qt_out[hd, 0:LANES, :] = qn[:, :LANES].T.astype(BF16)
        qt_out[hd, LANES:, :] = _rope_slab(qn[:, LANES:], cos_t, sin_a, sin_b).T.astype(BF16)
        vh = kv[:, MLA_HEADS * MLA_NOPE + hd * MLA_V:MLA_HEADS * MLA_NOPE + (hd + 1) * MLA_V]
        vt_out[hd, 0, 0:MLA_V, :] = vh.T.astype(BF16)
        vt_out[hd, 0, MLA_V:, :] = ones
        kn = kv[:, hd * MLA_NOPE:(hd + 1) * MLA_NOPE]
        r_k = lax.rsqrt((jnp.sum(kn * kn, axis=-1, keepdims=True) + ss_kr) * (1.0 / MLA_QK) + EPS)
        k_out[:, hd * HEAD_PAD:hd * HEAD_PAD + LANES] = (kn * r_k * gkn_ref[...]).astype(BF16)
        k_out[:, hd * HEAD_PAD + LANES:(hd + 1) * HEAD_PAD] = (kr_rot * r_k).astype(BF16)


def _mla_pre(l, h, rope, wa, g_qa, g_kva, wq, wkv, gq, gkn, gkr):
    tm = ROW_TILE
    row = lambda w: pl.BlockSpec((tm, w), lambda i: (i, 0))
    return pl.pallas_call(
        _mla_pre_kernel,
        grid=(SEQ // tm,),
        in_specs=[row(D_MODEL), _rope_spec(0)]
        + [_lay(a, l) for a in (wa, g_qa, g_kva, wq, wkv, gq, gkn, gkr)],
        out_specs=[pl.BlockSpec((MLA_HEADS, HEAD_PAD, tm), lambda i: (0, 0, i)),
                   row(MLA_HEADS * HEAD_PAD),
                   pl.BlockSpec((MLA_HEADS, 1, MLA_V + V_ONES, tm), lambda i: (0, i, 0, 0))],
        out_shape=[jax.ShapeDtypeStruct((MLA_HEADS, HEAD_PAD, SEQ), BF16),
                   jax.ShapeDtypeStruct((SEQ, MLA_HEADS * HEAD_PAD), BF16),
                   jax.ShapeDtypeStruct((MLA_HEADS, SEQ // tm, MLA_V + V_ONES, tm), BF16)],
        compiler_params=_params(("parallel",)),
        name="mla_pre",
    )(h, rope, wa, g_qa, g_kva, wq, wkv, gq, gkn, gkr)


def _flash_kernel(qt_ref, k_ref, vt_ref, o_ref, m_scr, acc_scr, st_a, st_b):
    t = ATT_TILE
    i = pl.program_id(1)
    sub = SUBLANES
    pack = 2 * SUBLANES
    vrows = MLA_V + V_ONES
    m_scr[...] = jnp.full(m_scr.shape, MASK_VALUE, F32)
    acc_scr[...] = jnp.zeros(acc_scr.shape, F32)

    def logits(j, dst):
        start = pl.multiple_of(j * t, t)
        for hh in range(ATT_HEADS_PER_STEP):
            dst[hh] = _dot(k_ref[pl.ds(start, t), hh * HEAD_PAD:(hh + 1) * HEAD_PAD], qt_ref[hh])

    def softmax_pv(src, j, masked):
        if masked:
            key = lax.broadcasted_iota(jnp.int32, (t, t), 0)
            qry = lax.broadcasted_iota(jnp.int32, (t, t), 1)
            keep = key <= qry
        for hh in range(ATT_HEADS_PER_STEP):
            st = src[hh]
            if masked:
                st = jnp.where(keep, st, MASK_VALUE)
            s3 = st.reshape(t // pack, pack, t)
            mx = jnp.max(s3, axis=0)
            mx = jnp.maximum(mx[:sub], mx[sub:])
            for shift in (4, 2, 1):
                mx = jnp.maximum(mx, pltpu.roll(mx, shift, axis=0))
            m_prev = m_scr[hh]
            m_new = jnp.maximum(m_prev, mx)
            alpha = jnp.exp2(m_prev - m_new)
            m2 = jnp.concatenate([m_new, m_new], axis=0)
            p = jnp.exp2((s3 - m2[None]).astype(BF16)).reshape(t, t)
            pv = _dot(vt_ref[hh, j], p)
            acc = acc_scr[hh].reshape(vrows // sub, sub, t) * alpha[None] + pv.reshape(vrows // sub, sub, t)
            acc_scr[hh] = acc.reshape(vrows, t)
            m_scr[hh] = m_new

    logits(0, st_a)

    def pair(m, carry):
        logits(2 * m + 1, st_b)
        softmax_pv(st_a, 2 * m, False)
        logits(2 * m + 2, st_a)
        softmax_pv(st_b, 2 * m + 1, False)
        return carry

    lax.fori_loop(0, lax.shift_right_logical(i, 1), pair, 0)
    odd = (i & 1) == 1

    @pl.when(odd)
    def _():
        logits(i, st_b)
        softmax_pv(st_a, i - 1, False)
        softmax_pv(st_b, i, True)

    @pl.when(jnp.logical_not(odd))
    def _():
        softmax_pv(st_a, i, True)

    for hh in range(ATT_HEADS_PER_STEP):
        num = acc_scr[hh, 0:MLA_V, :].reshape(MLA_V // sub, sub, t)
        den = acc_scr[hh, MLA_V:MLA_V + sub, :]
        out_t = (num / den[None]).reshape(MLA_V, t)
        o_ref[:, hh * MLA_V:(hh + 1) * MLA_V] = out_t.T.astype(o_ref.dtype)


def _flash(qt, k, vt):
    t, hp = ATT_TILE, ATT_HEADS_PER_STEP
    vrows = MLA_V + V_ONES
    return pl.pallas_call(
        _flash_kernel,
        grid=(MLA_HEADS // hp, SEQ // t),
        in_specs=[pl.BlockSpec((hp, HEAD_PAD, t), lambda h, i: (h, 0, i)),
                  pl.BlockSpec((SEQ, hp * HEAD_PAD), lambda h, i: (0, h), pipeline_mode=pl.Buffered(1)),
                  pl.BlockSpec((hp, SEQ // t, vrows, t), lambda h, i: (h, 0, 0, 0),
                               pipeline_mode=pl.Buffered(1))],
        out_specs=pl.BlockSpec((t, hp * MLA_V), lambda h, i: (i, h)),
        out_shape=jax.ShapeDtypeStruct((SEQ, MLA_HEADS * MLA_V), BF16),
        scratch_shapes=[pltpu.VMEM((hp, SUBLANES, t), F32), pltpu.VMEM((hp, vrows, t), F32),
                        pltpu.VMEM((hp, t, t), F32), pltpu.VMEM((hp, t, t), F32)],
        compiler_params=_params(("parallel", "arbitrary")),
        name="flash",
    )(qt, k, vt)


def _sconv_kernel(h_ref, w_ref, wsc_ref, o_ref, buf):
    tm = ROW_TILE
    z = _rdot(h_ref[...], w_ref[...])

    @pl.when(pl.program_id(0) == 0)
    def _():
        buf[0:SUBLANES, :] = jnp.zeros((SUBLANES, GROUP), F32)

    g = z[:, GROUP:2 * GROUP] * z[:, 2 * GROUP:]
    buf[SUBLANES:, :] = g
    w = wsc_ref[...]
    conv = (w[2:3, :] * g + w[1:2, :] * buf[SUBLANES - 1:SUBLANES - 1 + tm, :]
            + w[0:1, :] * buf[SUBLANES - 2:SUBLANES - 2 + tm, :])
    o_ref[...] = (z[:, :GROUP] * conv).astype(o_ref.dtype)
    buf[0:SUBLANES, :] = buf[tm:tm + SUBLANES, :]


def _sconv(l, h, w_in, wsc, cast=()):
    tm = ROW_TILE
    n = SEQ // tm
    c_arr, c_in, c_out, c_shape = _cast_specs(cast, n)
    return pl.pallas_call(
        _with_casts(_sconv_kernel, 3, 1, len(cast)),
        grid=(n,),
        in_specs=[pl.BlockSpec((tm, D_MODEL), lambda i: (i, 0)),
                  _w_in_spec(l, "sconv"), _lay(wsc, l)] + c_in,
        out_specs=[pl.BlockSpec((tm, GROUP), lambda i: (i, 0))] + c_out,
        out_shape=[jax.ShapeDtypeStruct((SEQ, GROUP), BF16)] + c_shape,
        scratch_shapes=[pltpu.VMEM((tm + SUBLANES, GROUP), F32)],
        compiler_params=_params(("arbitrary",)),
        name="sconv",
    )(h, w_in, wsc, *c_arr)


def _conformer_kernel(h_ref, w_ref, wcf_ref, bcf_ref, gln_ref, bln_ref, o_ref, buf, wbuf, uscr):
    tm = ROW_TILE
    nslab = GROUP // LANES
    z = _rdot(h_ref[...], w_ref[...])

    @pl.when(pl.program_id(0) == 0)
    def _():
        buf[:, 0:CF_HALO, :] = jnp.zeros((nslab, CF_HALO, LANES), F32)

    glu = z[:, :GROUP] * jax.nn.sigmoid(z[:, GROUP:])
    for sl in range(nslab):
        buf[sl, CF_HALO:, :] = glu[:, sl * LANES:(sl + 1) * LANES]
    base = CF_HALO - (CF_K - 1)
    for phase in range(1, SUBLANES):
        wbuf[phase - 1] = buf[:, phase:phase + tm + CF_HALO - SUBLANES, :]

    def tap_rows(o, r0, sl):
        phase = o % SUBLANES
        start = pl.multiple_of(r0 + (o - phase), SUBLANES)
        if phase == 0:
            return buf[sl, pl.ds(start, CF_ROWS), :]
        return wbuf[phase - 1, sl, pl.ds(start, CF_ROWS), :]

    def row_block(rb, carry):
        r0 = pl.multiple_of(rb * CF_ROWS, CF_ROWS)
        for sl in range(nslab):
            lanes = slice(sl * LANES, (sl + 1) * LANES)
            acc = jnp.broadcast_to(bcf_ref[:, lanes], (CF_ROWS, LANES))
            for j in range(CF_K):
                acc = acc + wcf_ref[j:j + 1, lanes] * tap_rows(base + j, r0, sl)
            uscr[pl.ds(r0, CF_ROWS), lanes] = acc
        return carry

    lax.fori_loop(0, tm // CF_ROWS, row_block, 0)
    u = uscr[...]
    mu = jnp.mean(u, axis=-1, keepdims=True)
    uc = u - mu
    y = uc * lax.rsqrt(jnp.mean(uc * uc, axis=-1, keepdims=True) + EPS)
    y = y * gln_ref[...] + bln_ref[...]
    o_ref[...] = (y * jax.nn.sigmoid(y)).astype(o_ref.dtype)
    buf[:, 0:CF_HALO, :] = buf[:, tm:tm + CF_HALO, :]


def _conformer(l, h, w_in, wcf, bcf, gln, bln, cast=()):
    tm = ROW_TILE
    n = SEQ // tm
    c_arr, c_in, c_out, c_shape = _cast_specs(cast, n)
    return pl.pallas_call(
        _with_casts(_conformer_kernel, 6, 1, len(cast)),
        grid=(n,),
        in_specs=[pl.BlockSpec((tm, D_MODEL), lambda i: (i, 0)), _w_in_spec(l, "conformer")]
        + [_lay(a, l) for a in (wcf, bcf, gln, bln)] + c_in,
        out_specs=[pl.BlockSpec((tm, GROUP), lambda i: (i, 0))] + c_out,
        out_shape=[jax.ShapeDtypeStruct((SEQ, GROUP), BF16)] + c_shape,
        scratch_shapes=[pltpu.VMEM((GROUP // LANES, tm + CF_HALO, LANES), F32),
                        pltpu.VMEM((SUBLANES - 1, GROUP // LANES, tm + CF_HALO - SUBLANES, LANES), F32),
                        pltpu.VMEM((tm, GROUP), F32)],
        compiler_params=_params(("arbitrary",)),
        name="conformer",
    )(h, w_in, wcf, bcf, gln, bln, *c_arr)


def _retention_kernel(h_ref, rope_ref, wqk_ref, wv_ref, wg_ref, qmask_ref, kdec_ref, qdec_ref,
                      inner_ref, cdec_ref, gret_ref, o_ref, state):
    tm = ROW_TILE
    c = RET_CHUNK
    dk2 = RET_HEADS * RET_DK

    @pl.when(pl.program_id(0) == 0)
    def _():
        state[...] = jnp.zeros(state.shape, F32)

    half = RET_DK // 2

    for ci in range(tm // c):
        rows = slice(ci * c, (ci + 1) * c)
        hb = h_ref[rows, :]
        zqk = _dot(hb, wqk_ref[...])
        v = _dot(hb, wv_ref[...])
        gate = _dot(hb, wg_ref[...])
        cos_t, sin_a, sin_b = (rope_ref[rows, n * LANES:(n + 1) * LANES] for n in range(3))

        def rot(t):
            return jnp.concatenate(
                [x * cos_t + pltpu.roll(x, LANES - half, axis=1) * sin_a + pltpu.roll(x, half, axis=1) * sin_b
                 for x in (t[:, :LANES], t[:, LANES:])], axis=1)

        qc = rot(zqk[:, :dk2])
        kt = (rot(zqk[:, dk2:]) * (RET_DK ** -0.5)).T
        kt_b = kt.astype(BF16)
        for hd in range(RET_HEADS):
            cols = slice(hd * RET_DV, (hd + 1) * RET_DV)
            qm = (qc * qmask_ref[hd]).astype(BF16)
            vh = v[:, cols].astype(BF16)
            sc = _dot(qm, kt_b) * inner_ref[hd]
            st = state[hd]
            o = _dot(sc.astype(BF16), vh) + _dot(qm, st.astype(BF16)) * qdec_ref[hd]
            state[hd] = st * cdec_ref[hd] + _dot((kt * kdec_ref[hd]).astype(BF16), vh)
            on = o * _rms_scale(o, RET_DV) * gret_ref[hd]
            gh = gate[:, cols]
            o_ref[rows, cols] = (gh * jax.nn.sigmoid(gh) * on).astype(o_ref.dtype)


def _retention(l, h, rope, w_in, qmask, kdec, qdec, inner, cdec, gret, cast=()):
    tm = ROW_TILE
    n = SEQ // tm
    c_arr, c_in, c_out, c_shape = _cast_specs(cast, n)
    return pl.pallas_call(
        _with_casts(_retention_kernel, 11, 1, len(cast)),
        grid=(n,),
        in_specs=[pl.BlockSpec((tm, D_MODEL), lambda i: (i, 0)), _rope_spec(1), _w_in_spec(l, "ret_qk"),
                  _w_in_spec(l, "ret_v"), _w_in_spec(l, "ret_gate")]
        + [_const(a) for a in (qmask, kdec, qdec, inner, cdec)] + [_lay(gret, l)] + c_in,
        out_specs=[pl.BlockSpec((tm, GROUP), lambda i: (i, 0))] + c_out,
        out_shape=[jax.ShapeDtypeStruct((SEQ, GROUP), BF16)] + c_shape,
        scratch_shapes=[pltpu.VMEM((RET_HEADS, RET_HEADS * RET_DK, RET_DV), F32)],
        compiler_params=_params(("arbitrary",)),
        name="retention",
    )(h, rope, w_in, w_in, w_in, qmask, kdec, qdec, inner, cdec, gret, *c_arr)


def _out_proj_kernel(x_ref, ya_ref, yb_ref, yc_ref, yd_ref, wo_ref, g_ref, o_ref, h_out):
    acc = x_ref[...]
    for n, y_ref in enumerate((ya_ref, yb_ref, yc_ref, yd_ref)):
        acc = acc + _dot(y_ref[...], wo_ref[n * GROUP:(n + 1) * GROUP, :])
    o_ref[...] = acc
    h_out[...] = _normed_bf16(acc, g_ref)


def _out_proj(l, x, ya, yb, yc, yd, wo, g_ffn):
    tm = ROW_TILE
    row = lambda n: pl.BlockSpec((tm, n), lambda i: (i, 0))
    return pl.pallas_call(
        _out_proj_kernel,
        grid=(SEQ // tm,),
        in_specs=[row(D_MODEL), row(GROUP), row(GROUP), row(GROUP), row(GROUP), _lay(wo, l),
                  _lay(g_ffn, l)],
        out_specs=[row(D_MODEL), row(D_MODEL)],
        out_shape=[jax.ShapeDtypeStruct((SEQ, D_MODEL), F32), jax.ShapeDtypeStruct((SEQ, D_MODEL), BF16)],
        compiler_params=_params(("parallel",)),
        name="out_proj",
    )(x, ya, yb, yc, yd, wo, g_ffn)


def _ffn_kernel(x_ref, h_ref, wg_ref, wu_ref, cg_ref, cu_ref, wd_ref, o_ref, ubuf, carry):
    tm = ROW_TILE
    i = pl.program_id(0)
    j = pl.program_id(1)

    @pl.when(j == 0)
    def _():
        o_ref[...] = x_ref[...]

    @pl.when(i == 0)
    def _():
        carry[j] = jnp.zeros(carry.shape[1:], F32)

    ts = FF_TILE // FF_SUB
    h = h_ref[...]

    def conv(part, s, u, c_ref):
        cols = slice(s * ts, (s + 1) * ts)
        ubuf[part, s, 0:SUBLANES, :] = carry[j, part, :, cols]
        ubuf[part, s, SUBLANES:, :] = u
        carry[j, part, :, cols] = u[tm - SUBLANES:, :]
        w = c_ref[:, cols]
        return (w[2:3, :] * u + w[1:2, :] * ubuf[part, s, SUBLANES - 1:SUBLANES - 1 + tm, :]
                + w[0:1, :] * ubuf[part, s, SUBLANES - 2:SUBLANES - 2 + tm, :])

    us = [(_rdot(h, wg_ref[:, s * ts:(s + 1) * ts]), _rdot(h, wu_ref[:, s * ts:(s + 1) * ts]))
          for s in range(FF_SUB)]
    for s, (ug, uu) in enumerate(us):
        gate = conv(0, s, ug, cg_ref)
        up = conv(1, s, uu, cu_ref)
        act = (gate * jax.nn.sigmoid(gate) * up).astype(BF16)
        o_ref[...] += _dot(act, wd_ref[s * ts:(s + 1) * ts, :])


def _ffn(l, x, h, wg, wu, wd, w_conv):
    tm, tf = ROW_TILE, FF_TILE
    nf = D_FF // tf
    return pl.pallas_call(
        _ffn_kernel,
        grid=(SEQ // tm, nf),
        in_specs=[pl.BlockSpec((tm, D_MODEL), lambda i, j: (i, 0)),
                  pl.BlockSpec((tm, D_MODEL), lambda i, j: (i, 0)),
                  pl.BlockSpec((D_MODEL, tf), lambda i, j: (0, j)),
                  pl.BlockSpec((D_MODEL, tf), lambda i, j: (0, j)),
                  pl.BlockSpec((None, FFN_K, tf), lambda i, j: (l, 0, j)),
                  pl.BlockSpec((None, FFN_K, tf), lambda i, j: (l, 0, j + nf)),
                  pl.BlockSpec((tf, D_MODEL), lambda i, j: (j, 0))],
        out_specs=pl.BlockSpec((tm, D_MODEL), lambda i, j: (i, 0)),
        out_shape=jax.ShapeDtypeStruct((SEQ, D_MODEL), F32),
        scratch_shapes=[pltpu.VMEM((2, FF_SUB, tm + SUBLANES, tf // FF_SUB), F32),
                        pltpu.VMEM((nf, 2, SUBLANES, tf), F32)],
        compiler_params=_params(("arbitrary", "arbitrary")),
        name="ffn",
    )(x, h, wg, wu, w_conv, w_conv, wd)


def _ple_kernel(x_ref, p_ref, g_ref, wpg_ref, wpe_ref, *rest):
    o_ref = rest[-2] if len(rest) == 3 else rest[-1]
    hn = _normed_bf16(x_ref[...], g_ref)
    pb = p_ref[...].astype(BF16)
    for n in range(D_MODEL // GROUP):
        cols = slice(n * GROUP, (n + 1) * GROUP)
        gate = jax.nn.sigmoid(_dot(hn, wpg_ref[:, cols]))
        o_ref[:, cols] = x_ref[:, cols] + _dot(pb, wpe_ref[:, cols]) * gate
    if len(rest) == 3:
        gnext_ref, _, h_out = rest
        h_out[...] = _normed_bf16(o_ref[...], gnext_ref)


def _ple(l, x, p, g, wpg, wpe, g_mix):
    tm = ROW_TILE
    row = lambda n: pl.BlockSpec((tm, n), lambda i: (i, 0))
    last = l == DEPTH - 1
    in_specs = [row(D_MODEL), pl.BlockSpec((None, None, tm, PLE_DIM), lambda i: (l, 0, i, 0)),
                _lay(g, l), _lay(wpg, l), _lay(wpe, l)]
    out_specs = [row(D_MODEL)]
    out_shape = [jax.ShapeDtypeStruct((SEQ, D_MODEL), F32)]
    args = [x, p, g, wpg, wpe]
    if not last:
        in_specs.append(_lay(g_mix, l + 1))
        out_specs.append(row(D_MODEL))
        out_shape.append(jax.ShapeDtypeStruct((SEQ, D_MODEL), BF16))
        args.append(g_mix)
    outs = pl.pallas_call(
        _ple_kernel,
        grid=(SEQ // tm,),
        in_specs=in_specs,
        out_specs=out_specs,
        out_shape=out_shape,
        compiler_params=_params(("parallel",)),
        name="ple",
    )(*args)
    return (outs[0], None) if last else (outs[0], outs[1])


def _retention_tables():
    c = RET_CHUNK
    lg = jnp.log(1.0 - 2.0 ** (-5.0 - jnp.arange(RET_HEADS, dtype=F32)))
    idx = jnp.arange(c, dtype=F32)
    rel = idx[:, None] - idx[None, :]
    inner = jnp.where(rel[None] >= 0, jnp.exp(lg[:, None, None] * rel[None]), 0.0)
    qdec = jnp.exp(lg[:, None] * (idx + 1.0))[:, :, None]
    kdec_t = jnp.exp(lg[:, None] * (c - 1.0 - idx))
    cdec = jnp.exp(lg * c)
    lane = np.arange(RET_HEADS * RET_DK)
    head_of_lane = lane // RET_DK
    hmask = jnp.asarray(head_of_lane[None, :] == np.arange(RET_HEADS)[:, None], F32)
    qmask = hmask[:, None, :]
    kdec = hmask[:, :, None] * kdec_t[:, None, :]
    cdec_b = jnp.broadcast_to(cdec[:, None, None], (RET_HEADS, 1, RET_DV))
    return qmask, kdec, qdec, inner, cdec_b


def kernel(x, p, positions, g_mix, w_in, g_qa, g_kva, w_q_up, w_kv_up, g_qn, g_kn, w_sc, w_cf,
           b_cf, g_cf_ln, b_cf_ln, g_ret, w_o, g_ffn, w_up, w_ffn_conv, w_down, g_pe, w_pe, w_pg):
    assert x.shape == (1, SEQ, D_MODEL)
    xs = x[0]
    posf = positions.astype(F32)
    rows = lambda a: a[:, None, :]

    mla_freq = (ROPE_BASE ** (-jnp.arange(0, MLA_ROPE, 2, dtype=F32) / MLA_ROPE))[:, None]
    ret_freq = (1.0 / (10000.0 ** jnp.linspace(0.0, 1.0, RET_DK // 2, dtype=F32)))[:, None]
    qmask, kdec, qdec, inner, cdec = _retention_tables()

    w_in_r = lax.dynamic_update_slice(jnp.zeros((DEPTH, D_MODEL, w_in.shape[-1] + W_IN_SHIFT), BF16),
                                      w_in.astype(BF16), (0, 0, W_IN_SHIFT))
    wa = w_in[:, :, :MLA_IN].astype(BF16)
    wq = jnp.pad(w_q_up.reshape(DEPTH, Q_LORA, MLA_HEADS, MLA_QK),
                 ((0, 0), (0, 0), (0, 0), (0, HEAD_PAD - MLA_QK))
                 ).reshape(DEPTH, Q_LORA, MLA_HEADS * HEAD_PAD).astype(BF16)
    wkv4 = w_kv_up.reshape(DEPTH, KV_LORA, MLA_HEADS, MLA_NOPE + MLA_V)
    wkv = jnp.concatenate([wkv4[..., :MLA_NOPE].reshape(DEPTH, KV_LORA, -1),
                           wkv4[..., MLA_NOPE:].reshape(DEPTH, KV_LORA, -1)], axis=2).astype(BF16)
    gq = jnp.pad(g_qn * (MLA_QK ** -0.5 * LOG2_E), ((0, 0), (0, HEAD_PAD - MLA_QK)))
    gkr = jnp.pad(g_kn[:, MLA_NOPE:], ((0, 0), (0, LANES - MLA_ROPE)))
    gm, g_ret4 = rows(g_mix), g_ret[:, :, None, :]

    hm, rope = _prologue(xs, gm, posf, mla_freq, ret_freq)
    for l in range(DEPTH):
        qt, k, vt = _mla_pre(l, hm, rope, wa, rows(g_qa), rows(g_kva), wq, wkv, rows(gq),
                             rows(g_kn[:, :MLA_NOPE]), rows(gkr))
        y_a = _flash(qt, k, vt)
        first = l == 0
        y_b, wd, *cast_b = _sconv(l, hm, w_in_r, w_sc,
                                  cast=((w_down, l, 1, 0),) + ((w_o,) if first else ()))
        y_c, *cast_c = _conformer(l, hm, w_in_r, w_cf, rows(b_cf), rows(g_cf_ln), rows(b_cf_ln),
                                  cast=(w_pg, w_pe) if first else ())
        if first:
            (w_o_b,), (w_pg_b, w_pe_b) = cast_b, cast_c
        y_d, wg, wu = _retention(l, hm, rope, w_in_r, qmask, kdec, qdec, inner, cdec, g_ret4,
                                 cast=((w_up, l, 2, 0), (w_up, l, 2, 1)))
        xs, hf = _out_proj(l, xs, y_a, y_b, y_c, y_d, w_o_b, rows(g_ffn))
        xs = _ffn(l, xs, hf, wg, wu, wd, w_ffn_conv)
        xs, hm = _ple(l, xs, p, rows(g_pe), w_pg_b, w_pe_b, gm)
    return xs[None]
```

```python
import numpy as np
import jax
import jax.numpy as jnp
from jax import lax
from jax.experimental import pallas as pl
from jax.experimental.pallas import tpu as pltpu

F32 = jnp.float32
BF16 = jnp.bfloat16

D_MODEL = 2048
SEQ = 8192
DEPTH = 2
GROUP = 512
MLA_HEADS = 4
MLA_NOPE = 128
MLA_ROPE = 64
MLA_V = 128
MLA_QK = MLA_NOPE + MLA_ROPE
Q_LORA = 512
KV_LORA = 256
ROPE_BASE = 10000.0
SC_K = 3
CF_K = 31
RET_HEADS = 4
RET_DV = 128
RET_DK = 64
RET_CHUNK = 128
D_FF = 5632
FFN_K = 3
PLE_DIM = 256
EPS = 1e-6

LANES = 128
SUBLANES = 8
HEAD_PAD = 2 * LANES
ROW_TILE = 512
FF_TILE = 512
FF_SUB = 2
DOT_ROWS = 128
ATT_TILE = 512
ATT_HEADS_PER_STEP = 4
V_ONES = 16
CF_HALO = 32
CF_ROWS = 128
MLA_IN = Q_LORA + KV_LORA + LANES
W_IN_SHIFT = 3 * GROUP - (Q_LORA + KV_LORA + MLA_ROPE)
W_IN_GROUPS = {"sconv": (3 * GROUP, 1), "conformer": (2 * GROUP, 3), "ret_qk": (GROUP, 8),
               "ret_v": (GROUP, 9), "ret_gate": (GROUP, 10)}
VMEM_LIMIT = 56 * 1024 * 1024
MASK_VALUE = -1e30
LOG2_E = 1.4426950408889634


def _lay(a, l):
    return pl.BlockSpec((None,) + a.shape[1:], lambda *_: (l,) + (0,) * (a.ndim - 1))


def _const(a):
    return pl.BlockSpec(a.shape, lambda *_: (0,) * a.ndim)


def _w_in_spec(l, group):
    width, block = W_IN_GROUPS[group]
    return pl.BlockSpec((None, D_MODEL, width), lambda *_: (l, 0, block))


def _params(sem):
    return pltpu.CompilerParams(dimension_semantics=sem, vmem_limit_bytes=VMEM_LIMIT)


def _rms_scale(xf, n):
    return lax.rsqrt(jnp.sum(xf * xf, axis=-1, keepdims=True) * (1.0 / n) + EPS)


def _normed_bf16(xf, g_ref):
    return (xf * _rms_scale(xf, D_MODEL) * g_ref[...]).astype(BF16)


def _with_casts(body, n_in, n_out, n_cast):
    def kernel_fn(*refs):
        ins, srcs = refs[:n_in], refs[n_in:n_in + n_cast]
        outs = refs[n_in + n_cast:n_in + n_cast + n_out]
        dsts = refs[n_in + n_cast + n_out:n_in + 2 * n_cast + n_out]
        for src, dst in zip(srcs, dsts):
            dst[...] = src[...].astype(BF16)
        body(*ins, *outs, *refs[n_in + 2 * n_cast + n_out:])
    return kernel_fn


def _cast_specs(items, steps):
    arrays, ins, outs, shapes = [], [], [], []
    for item in items:
        if isinstance(item, tuple):
            a, layer, ncol, col = item
            rows, cols = a.shape[-2] // steps, a.shape[-1] // ncol
            ins.append(pl.BlockSpec((None, rows, cols), lambda i, layer=layer, col=col: (layer, i, col)))
            outs.append(pl.BlockSpec((rows, cols), lambda i: (i, 0)))
            shapes.append(jax.ShapeDtypeStruct((a.shape[-2], cols), BF16))
        else:
            a = item
            blk = a.shape[:-2] + (a.shape[-2] // steps, a.shape[-1])
            idx = lambda i, nd=a.ndim: (0,) * (nd - 2) + (i, 0)
            ins.append(pl.BlockSpec(blk, idx))
            outs.append(pl.BlockSpec(blk, idx))
            shapes.append(jax.ShapeDtypeStruct(a.shape, BF16))
        arrays.append(a)
    return arrays, ins, outs, shapes


def _dot(a, b):
    return jnp.dot(a, b, preferred_element_type=F32)


def _rdot(a, b):
    n = a.shape[0] // DOT_ROWS
    return jnp.concatenate([_dot(a[r * DOT_ROWS:(r + 1) * DOT_ROWS], b) for r in range(n)], axis=0)


def _prologue_kernel(x_ref, g_ref, pos_ref, fm_ref, fr_ref, h_out, o_ref):
    h_out[...] = _normed_bf16(x_ref[...], g_ref)
    pos = pos_ref[...]
    lane = lax.broadcasted_iota(jnp.int32, (LANES, pos.shape[1]), 0)
    for n, (f_ref, is_x1, is_x2) in enumerate((
            (fm_ref, lane < MLA_ROPE // 2, (lane >= MLA_ROPE // 2) & (lane < MLA_ROPE)),
            (fr_ref, (lane % RET_DK) < RET_DK // 2, (lane % RET_DK) >= RET_DK // 2))):
        ang = f_ref[...] * pos
        reps = LANES // ang.shape[0]
        cos_t = jnp.concatenate([jnp.cos(ang)] * reps, axis=0)
        sin_t = jnp.concatenate([jnp.sin(ang)] * reps, axis=0)
        o_ref[:, (3 * n) * LANES:(3 * n + 1) * LANES] = cos_t.T
        o_ref[:, (3 * n + 1) * LANES:(3 * n + 2) * LANES] = jnp.where(is_x1, -sin_t, 0.0).T
        o_ref[:, (3 * n + 2) * LANES:(3 * n + 3) * LANES] = jnp.where(is_x2, sin_t, 0.0).T


def _prologue(x, g_mix, posf, mla_freq, ret_freq):
    tm = ROW_TILE
    row = pl.BlockSpec((tm, D_MODEL), lambda i: (i, 0))
    return pl.pallas_call(
        _prologue_kernel,
        grid=(SEQ // tm,),
        in_specs=[row, _lay(g_mix, 0), pl.BlockSpec((1, tm), lambda i: (0, i)), _const(mla_freq),
                  _const(ret_freq)],
        out_specs=[row, pl.BlockSpec((tm, 6 * LANES), lambda i: (i, 0))],
        out_shape=[jax.ShapeDtypeStruct((SEQ, D_MODEL), BF16), jax.ShapeDtypeStruct((SEQ, 6 * LANES), F32)],
        compiler_params=_params(("parallel",)),
        name="prologue",
    )(x, g_mix, posf, mla_freq, ret_freq)


def _rope_spec(which):
    return pl.BlockSpec((ROW_TILE, 3 * LANES), lambda i: (i, which))


def _rope_slab(x, cos_t, sin_a, sin_b):
    return (x * cos_t + pltpu.roll(x, LANES - 32, axis=1) * sin_a
            + pltpu.roll(x, 32, axis=1) * sin_b)


def _mla_pre_kernel(h_ref, rope_ref, wa_ref, gqa_ref, gkva_ref, wq_ref, wkv_ref,
                    gq_ref, gkn_ref, gkr_ref, qt_out, k_out, vt_out):
    z = _rdot(h_ref[...], wa_ref[...])
    zq = z[:, :Q_LORA]
    zkv = z[:, Q_LORA:Q_LORA + KV_LORA]
    lane = lax.broadcasted_iota(jnp.int32, (z.shape[0], LANES), 1)
    zkr = jnp.where(lane < MLA_ROPE, z[:, Q_LORA + KV_LORA:], 0.0)
    c_q = (zq * _rms_scale(zq, Q_LORA) * gqa_ref[...]).astype(BF16)
    c_kv = (zkv * _rms_scale(zkv, KV_LORA) * gkva_ref[...]).astype(BF16)

    cos_t, sin_a, sin_b = (rope_ref[:, n * LANES:(n + 1) * LANES] for n in range(3))

    q = _rdot(c_q, wq_ref[...])
    kv = _rdot(c_kv, wkv_ref[...])

    kr = zkr * gkr_ref[...]
    ss_kr = jnp.sum(zkr * zkr, axis=-1, keepdims=True)
    kr_rot = _rope_slab(kr, cos_t, sin_a, sin_b)
    ones = jnp.ones((V_ONES, z.shape[0]), BF16)
    for hd in range(MLA_HEADS):
        qb = q[:, hd * HEAD_PAD:(hd + 1) * HEAD_PAD]
        qn = qb * _rms_scale(qb, MLA_QK) * gq_ref[...]
        qt_out[hd, 0:LANES, :] = qn[:, :LANES].T.astype(BF16)
        qt_out[hd, LANES:, :] = _rope_slab(qn[:, LANES:], cos_t, sin_a, sin_b).T.astype(BF16)
        vh = kv[:, MLA_HEADS * MLA_NOPE + hd * MLA_V:MLA_HEADS * MLA_NOPE + (hd + 1) * MLA_V]
        vt_out[hd, 0, 0:MLA_V, :] = vh.T.astype(BF16)
        vt_out[hd, 0, MLA_V:, :] = ones
        kn = kv[:, hd * MLA_NOPE:(hd + 1) * MLA_NOPE]
        r_k = lax.rsqrt((jnp.sum(kn * kn, axis=-1, keepdims=True) + ss_kr) * (1.0 / MLA_QK) + EPS)
        k_out[:, hd * HEAD_PAD:hd * HEAD_PAD + LANES] = (kn * r_k * gkn_ref[...]).astype(BF16)
        k_out[:, hd * HEAD_PAD + LANES:(hd + 1) * HEAD_PAD] = (kr_rot * r_k).astype(BF16)


def _mla_pre(l, h, rope, wa, g_qa, g_kva, wq, wkv, gq, gkn, gkr):
    tm = ROW_TILE
    row = lambda w: pl.BlockSpec((tm, w), lambda i: (i, 0))
    return pl.pallas_call(
        _mla_pre_kernel,
        grid=(SEQ // tm,),
        in_specs=[row(D_MODEL), _rope_spec(0)]
        + [_lay(a, l) for a in (wa, g_qa, g_kva, wq, wkv, gq, gkn, gkr)],
        out_specs=[pl.BlockSpec((MLA_HEADS, HEAD_PAD, tm), lambda i: (0, 0, i)),
                   row(MLA_HEADS * HEAD_PAD),
                   pl.BlockSpec((MLA_HEADS, 1, MLA_V + V_ONES, tm), lambda i: (0, i, 0, 0))],
        out_shape=[jax.ShapeDtypeStruct((MLA_HEADS, HEAD_PAD, SEQ), BF16),
                   jax.ShapeDtypeStruct((SEQ, MLA_HEADS * HEAD_PAD), BF16),
                   jax.ShapeDtypeStruct((MLA_HEADS, SEQ // tm, MLA_V + V_ONES, tm), BF16)],
        compiler_params=_params(("parallel",)),
        name="mla_pre",
    )(h, rope, wa, g_qa, g_kva, wq, wkv, gq, gkn, gkr)


def _flash_kernel(qt_ref, k_ref, vt_ref, o_ref, m_scr, acc_scr, st_a, st_b):
    t = ATT_TILE
    i = pl.program_id(1)
    sub = SUBLANES
    pack = 2 * SUBLANES
    vrows = MLA_V + V_ONES
    m_scr[...] = jnp.full(m_scr.shape, MASK_VALUE, F32)
    acc_scr[...] = jnp.zeros(acc_scr.shape, F32)

    def logits(j, dst):
        start = pl.multiple_of(j * t, t)
        for hh in range(ATT_HEADS_PER_STEP):
            dst[hh] = _dot(k_ref[pl.ds(start, t), hh * HEAD_PAD:(hh + 1) * HEAD_PAD], qt_ref[hh])

    def softmax_pv(src, j, masked):
        if masked:
            key = lax.broadcasted_iota(jnp.int32, (t, t), 0)
            qry = lax.broadcasted_iota(jnp.int32, (t, t), 1)
            keep = key <= qry
        for hh in range(ATT_HEADS_PER_STEP):
            st = src[hh]
            if masked:
                st = jnp.where(keep, st, MASK_VALUE)
            s3 = st.reshape(t // pack, pack, t)
            mx = jnp.max(s3, axis=0)
            mx = jnp.maximum(mx[:sub], mx[sub:])
            for shift in (4, 2, 1):
                mx = jnp.maximum(mx, pltpu.roll(mx, shift, axis=0))
            m_prev = m_scr[hh]
            m_new = jnp.maximum(m_prev, mx)
            alpha = jnp.exp2(m_prev - m_new)
            m2 = jnp.concatenate([m_new, m_new], axis=0)
            p = jnp.exp2((s3 - m2[None]).astype(BF16)).reshape(t, t)
            pv = _dot(vt_ref[hh, j], p)
            acc = acc_scr[hh].reshape(vrows // sub, sub, t) * alpha[None] + pv.reshape(vrows // sub, sub, t)
            acc_scr[hh] = acc.reshape(vrows, t)
            m_scr[hh] = m_new

    logits(0, st_a)

    def pair(m, carry):
        logits(2 * m + 1, st_b)
        softmax_pv(st_a, 2 * m, False)
        logits(2 * m + 2, st_a)
        softmax_pv(st_b, 2 * m + 1, False)
        return carry

    lax.fori_loop(0, lax.shift_right_logical(i, 1), pair, 0)
    odd = (i & 1) == 1

    @pl.when(odd)
    def _():
        logits(i, st_b)
        softmax_pv(st_a, i - 1, False)
        softmax_pv(st_b, i, True)

    @pl.when(jnp.logical_not(odd))
    def _():
        softmax_pv(st_a, i, True)

    for hh in range(ATT_HEADS_PER_STEP):
        num = acc_scr[hh, 0:MLA_V, :].reshape(MLA_V // sub, sub, t)
        den = acc_scr[hh, MLA_V:MLA_V + sub, :]
        out_t = (num / den[None]).reshape(MLA_V, t)
        o_ref[:, hh * MLA_V:(hh + 1) * MLA_V] = out_t.T.astype(o_ref.dtype)


def _flash(qt, k, vt):
    t, hp = ATT_TILE, ATT_HEADS_PER_STEP
    vrows = MLA_V + V_ONES
    return pl.pallas_call(
        _flash_kernel,
        grid=(MLA_HEADS // hp, SEQ // t),
        in_specs=[pl.BlockSpec((hp, HEAD_PAD, t), lambda h, i: (h, 0, i)),
                  pl.BlockSpec((SEQ, hp * HEAD_PAD), lambda h, i: (0, h), pipeline_mode=pl.Buffered(1)),
                  pl.BlockSpec((hp, SEQ // t, vrows, t), lambda h, i: (h, 0, 0, 0),
                               pipeline_mode=pl.Buffered(1))],
        out_specs=pl.BlockSpec((t, hp * MLA_V), lambda h, i: (i, h)),
        out_shape=jax.ShapeDtypeStruct((SEQ, MLA_HEADS * MLA_V), BF16),
        scratch_shapes=[pltpu.VMEM((hp, SUBLANES, t), F32), pltpu.VMEM((hp, vrows, t), F32),
                        pltpu.VMEM((hp, t, t), F32), pltpu.VMEM((hp, t, t), F32)],
        compiler_params=_params(("parallel", "arbitrary")),
        name="flash",
    )(qt, k, vt)


def _sconv_kernel(h_ref, w_ref, wsc_ref, o_ref, buf):
    tm = ROW_TILE
    z = _rdot(h_ref[...], w_ref[...])

    @pl.when(pl.program_id(0) == 0)
    def _():
        buf[0:SUBLANES, :] = jnp.zeros((SUBLANES, GROUP), F32)

    g = z[:, GROUP:2 * GROUP] * z[:, 2 * GROUP:]
    buf[SUBLANES:, :] = g
    w = wsc_ref[...]
    conv = (w[2:3, :] * g + w[1:2, :] * buf[SUBLANES - 1:SUBLANES - 1 + tm, :]
            + w[0:1, :] * buf[SUBLANES - 2:SUBLANES - 2 + tm, :])
    o_ref[...] = (z[:, :GROUP] * conv).astype(o_ref.dtype)
    buf[0:SUBLANES, :] = buf[tm:tm + SUBLANES, :]


def _sconv(l, h, w_in, wsc, cast=()):
    tm = ROW_TILE
    n = SEQ // tm
    c_arr, c_in, c_out, c_shape = _cast_specs(cast, n)
    return pl.pallas_call(
        _with_casts(_sconv_kernel, 3, 1, len(cast)),
        grid=(n,),
        in_specs=[pl.BlockSpec((tm, D_MODEL), lambda i: (i, 0)),
                  _w_in_spec(l, "sconv"), _lay(wsc, l)] + c_in,
        out_specs=[pl.BlockSpec((tm, GROUP), lambda i: (i, 0))] + c_out,
        out_shape=[jax.ShapeDtypeStruct((SEQ, GROUP), BF16)] + c_shape,
        scratch_shapes=[pltpu.VMEM((tm + SUBLANES, GROUP), F32)],
        compiler_params=_params(("arbitrary",)),
        name="sconv",
    )(h, w_in, wsc, *c_arr)


def _conformer_kernel(h_ref, w_ref, wcf_ref, bcf_ref, gln_ref, bln_ref, o_ref, buf, wbuf, uscr):
    tm = ROW_TILE
    nslab = GROUP // LANES
    z = _rdot(h_ref[...], w_ref[...])

    @pl.when(pl.program_id(0) == 0)
    def _():
        buf[:, 0:CF_HALO, :] = jnp.zeros((nslab, CF_HALO, LANES), F32)

    glu = z[:, :GROUP] * jax.nn.sigmoid(z[:, GROUP:])
    for sl in range(nslab):
        buf[sl, CF_HALO:, :] = glu[:, sl * LANES:(sl + 1) * LANES]
    base = CF_HALO - (CF_K - 1)
    for phase in range(1, SUBLANES):
        wbuf[phase - 1] = buf[:, phase:phase + tm + CF_HALO - SUBLANES, :]

    def tap_rows(o, r0, sl):
        phase = o % SUBLANES
        start = pl.multiple_of(r0 + (o - phase), SUBLANES)
        if phase == 0:
            return buf[sl, pl.ds(start, CF_ROWS), :]
        return wbuf[phase - 1, sl, pl.ds(start, CF_ROWS), :]

    def row_block(rb, carry):
        r0 = pl.multiple_of(rb * CF_ROWS, CF_ROWS)
        for sl in range(nslab):
            lanes = slice(sl * LANES, (sl + 1) * LANES)
            acc = jnp.broadcast_to(bcf_ref[:, lanes], (CF_ROWS, LANES))
            for j in range(CF_K):
                acc = acc + wcf_ref[j:j + 1, lanes] * tap_rows(base + j, r0, sl)
            uscr[pl.ds(r0, CF_ROWS), lanes] = acc
        return carry

    lax.fori_loop(0, tm // CF_ROWS, row_block, 0)
    u = uscr[...]
    mu = jnp.mean(u, axis=-1, keepdims=True)
    uc = u - mu
    y = uc * lax.rsqrt(jnp.mean(uc * uc, axis=-1, keepdims=True) + EPS)
    y = y * gln_ref[...] + bln_ref[...]
    o_ref[...] = (y * jax.nn.sigmoid(y)).astype(o_ref.dtype)
    buf[:, 0:CF_HALO, :] = buf[:, tm:tm + CF_HALO, :]


def _conformer(l, h, w_in, wcf, bcf, gln, bln, cast=()):
    tm = ROW_TILE
    n = SEQ // tm
    c_arr, c_in, c_out, c_shape = _cast_specs(cast, n)
    return pl.pallas_call(
        _with_casts(_conformer_kernel, 6, 1, len(cast)),
        grid=(n,),
        in_specs=[pl.BlockSpec((tm, D_MODEL), lambda i: (i, 0)), _w_in_spec(l, "conformer")]
        + [_lay(a, l) for a in (wcf, bcf, gln, bln)] + c_in,
        out_specs=[pl.BlockSpec((tm, GROUP), lambda i: (i, 0))] + c_out,
        out_shape=[jax.ShapeDtypeStruct((SEQ, GROUP), BF16)] + c_shape,
        scratch_shapes=[pltpu.VMEM((GROUP // LANES, tm + CF_HALO, LANES), F32),
                        pltpu.VMEM((SUBLANES - 1, GROUP // LANES, tm + CF_HALO - SUBLANES, LANES), F32),
                        pltpu.VMEM((tm, GROUP), F32)],
        compiler_params=_params(("arbitrary",)),
        name="conformer",
    )(h, w_in, wcf, bcf, gln, bln, *c_arr)


def _retention_kernel(h_ref, rope_ref, wqk_ref, wv_ref, wg_ref, qmask_ref, kdec_ref, qdec_ref,
                      inner_ref, cdec_ref, gret_ref, o_ref, state):
    tm = ROW_TILE
    c = RET_CHUNK
    dk2 = RET_HEADS * RET_DK

    @pl.when(pl.program_id(0) == 0)
    def _():
        state[...] = jnp.zeros(state.shape, F32)

    half = RET_DK // 2

    for ci in range(tm // c):
        rows = slice(ci * c, (ci + 1) * c)
        hb = h_ref[rows, :]
        zqk = _dot(hb, wqk_ref[...])
        v = _dot(hb, wv_ref[...])
        gate = _dot(hb, wg_ref[...])
        cos_t, sin_a, sin_b = (rope_ref[rows, n * LANES:(n + 1) * LANES] for n in range(3))

        def rot(t):
            return jnp.concatenate(
                [x * cos_t + pltpu.roll(x, LANES - half, axis=1) * sin_a + pltpu.roll(x, half, axis=1) * sin_b
                 for x in (t[:, :LANES], t[:, LANES:])], axis=1)

        qc = rot(zqk[:, :dk2])
        kt = (rot(zqk[:, dk2:]) * (RET_DK ** -0.5)).T
        kt_b = kt.astype(BF16)
        for hd in range(RET_HEADS):
            cols = slice(hd * RET_DV, (hd + 1) * RET_DV)
            qm = (qc * qmask_ref[hd]).astype(BF16)
            vh = v[:, cols].astype(BF16)
            sc = _dot(qm, kt_b) * inner_ref[hd]
            st = state[hd]
            o = _dot(sc.astype(BF16), vh) + _dot(qm, st.astype(BF16)) * qdec_ref[hd]
            state[hd] = st * cdec_ref[hd] + _dot((kt * kdec_ref[hd]).astype(BF16), vh)
            on = o * _rms_scale(o, RET_DV) * gret_ref[hd]
            gh = gate[:, cols]
            o_ref[rows, cols] = (gh * jax.nn.sigmoid(gh) * on).astype(o_ref.dtype)


def _retention(l, h, rope, w_in, qmask, kdec, qdec, inner, cdec, gret, cast=()):
    tm = ROW_TILE
    n = SEQ // tm
    c_arr, c_in, c_out, c_shape = _cast_specs(cast, n)
    return pl.pallas_call(
        _with_casts(_retention_kernel, 11, 1, len(cast)),
        grid=(n,),
        in_specs=[pl.BlockSpec((tm, D_MODEL), lambda i: (i, 0)), _rope_spec(1), _w_in_spec(l, "ret_qk"),
                  _w_in_spec(l, "ret_v"), _w_in_spec(l, "ret_gate")]
        + [_const(a) for a in (qmask, kdec, qdec, inner, cdec)] + [_lay(gret, l)] + c_in,
        out_specs=[pl.BlockSpec((tm, GROUP), lambda i: (i, 0))] + c_out,
        out_shape=[jax.ShapeDtypeStruct((SEQ, GROUP), BF16)] + c_shape,
        scratch_shapes=[pltpu.VMEM((RET_HEADS, RET_HEADS * RET_DK, RET_DV), F32)],
        compiler_params=_params(("arbitrary",)),
        name="retention",
    )(h, rope, w_in, w_in, w_in, qmask, kdec, qdec, inner, cdec, gret, *c_arr)


def _out_proj_kernel(x_ref, ya_ref, yb_ref, yc_ref, yd_ref, wo_ref, g_ref, o_ref, h_out):
    acc = x_ref[...]
    for n, y_ref in enumerate((ya_ref, yb_ref, yc_ref, yd_ref)):
        acc = acc + _dot(y_ref[...], wo_ref[n * GROUP:(n + 1) * GROUP, :])
    o_ref[...] = acc
    h_out[...] = _normed_bf16(acc, g_ref)


def _out_proj(l, x, ya, yb, yc, yd, wo, g_ffn):
    tm = ROW_TILE
    row = lambda n: pl.BlockSpec((tm, n), lambda i: (i, 0))
    return pl.pallas_call(
        _out_proj_kernel,
        grid=(SEQ // tm,),
        in_specs=[row(D_MODEL), row(GROUP), row(GROUP), row(GROUP), row(GROUP), _lay(wo, l),
                  _lay(g_ffn, l)],
        out_specs=[row(D_MODEL), row(D_MODEL)],
        out_shape=[jax.ShapeDtypeStruct((SEQ, D_MODEL), F32), jax.ShapeDtypeStruct((SEQ, D_MODEL), BF16)],
        compiler_params=_params(("parallel",)),
        name="out_proj",
    )(x, ya, yb, yc, yd, wo, g_ffn)


def _ffn_kernel(x_ref, h_ref, wg_ref, wu_ref, cg_ref, cu_ref, wd_ref, o_ref, ubuf, carry):
    tm = ROW_TILE
    i = pl.program_id(0)
    j = pl.program_id(1)

    @pl.when(j == 0)
    def _():
        o_ref[...] = x_ref[...]

    @pl.when(i == 0)
    def _():
        carry[j] = jnp.zeros(carry.shape[1:], F32)

    ts = FF_TILE // FF_SUB
    h = h_ref[...]

    def conv(part, s, u, c_ref):
        cols = slice(s * ts, (s + 1) * ts)
        ubuf[part, s, 0:SUBLANES, :] = carry[j, part, :, cols]
        ubuf[part, s, SUBLANES:, :] = u
        carry[j, part, :, cols] = u[tm - SUBLANES:, :]
        w = c_ref[:, cols]
        return (w[2:3, :] * u + w[1:2, :] * ubuf[part, s, SUBLANES - 1:SUBLANES - 1 + tm, :]
                + w[0:1, :] * ubuf[part, s, SUBLANES - 2:SUBLANES - 2 + tm, :])

    us = [(_rdot(h, wg_ref[:, s * ts:(s + 1) * ts]), _rdot(h, wu_ref[:, s * ts:(s + 1) * ts]))
          for s in range(FF_SUB)]
    for s, (ug, uu) in enumerate(us):
        gate = conv(0, s, ug, cg_ref)
        up = conv(1, s, uu, cu_ref)
        act = (gate * jax.nn.sigmoid(gate) * up).astype(BF16)
        o_ref[...] += _dot(act, wd_ref[s * ts:(s + 1) * ts, :])


def _ffn(l, x, h, wg, wu, wd, w_conv):
    tm, tf = ROW_TILE, FF_TILE
    nf = D_FF // tf
    return pl.pallas_call(
        _ffn_kernel,
        grid=(SEQ // tm, nf),
        in_specs=[pl.BlockSpec((tm, D_MODEL), lambda i, j: (i, 0)),
                  pl.BlockSpec((tm, D_MODEL), lambda i, j: (i, 0)),
                  pl.BlockSpec((D_MODEL, tf), lambda i, j: (0, j)),
                  pl.BlockSpec((D_MODEL, tf), lambda i, j: (0, j)),
                  pl.BlockSpec((None, FFN_K, tf), lambda i, j: (l, 0, j)),
                  pl.BlockSpec((None, FFN_K, tf), lambda i, j: (l, 0, j + nf)),
                  pl.BlockSpec((tf, D_MODEL), lambda i, j: (j, 0))],
        out_specs=pl.BlockSpec((tm, D_MODEL), lambda i, j: (i, 0)),
        out_shape=jax.ShapeDtypeStruct((SEQ, D_MODEL), F32),
        scratch_shapes=[pltpu.VMEM((2, FF_SUB, tm + SUBLANES, tf // FF_SUB), F32),
                        pltpu.VMEM((nf, 2, SUBLANES, tf), F32)],
        compiler_params=_params(("arbitrary", "arbitrary")),
        name="ffn",
    )(x, h, wg, wu, w_conv, w_conv, wd)


def _ple_kernel(x_ref, p_ref, g_ref, wpg_ref, wpe_ref, *rest):
    o_ref = rest[-2] if len(rest) == 3 else rest[-1]
    hn = _normed_bf16(x_ref[...], g_ref)
    pb = p_ref[...].astype(BF16)
    for n in range(D_MODEL // GROUP):
        cols = slice(n * GROUP, (n + 1) * GROUP)
        gate = jax.nn.sigmoid(_dot(hn, wpg_ref[:, cols]))
        o_ref[:, cols] = x_ref[:, cols] + _dot(pb, wpe_ref[:, cols]) * gate
    if len(rest) == 3:
        gnext_ref, _, h_out = rest
        h_out[...] = _normed_bf16(o_ref[...], gnext_ref)


def _ple(l, x, p, g, wpg, wpe, g_mix):
    tm = ROW_TILE
    row = lambda n: pl.BlockSpec((tm, n), lambda i: (i, 0))
    last = l == DEPTH - 1
    in_specs = [row(D_MODEL), pl.BlockSpec((None, None, tm, PLE_DIM), lambda i: (l, 0, i, 0)),
                _lay(g, l), _lay(wpg, l), _lay(wpe, l)]
    out_specs = [row(D_MODEL)]
    out_shape = [jax.ShapeDtypeStruct((SEQ, D_MODEL), F32)]
    args = [x, p, g, wpg, wpe]
    if not last:
        in_specs.append(_lay(g_mix, l + 1))
        out_specs.append(row(D_MODEL))
        out_shape.append(jax.ShapeDtypeStruct((SEQ, D_MODEL), BF16))
        args.append(g_mix)
    outs = pl.pallas_call(
        _ple_kernel,
        grid=(SEQ // tm,),
        in_specs=in_specs,
        out_specs=out_specs,
        out_shape=out_shape,
        compiler_params=_params(("parallel",)),
        name="ple",
    )(*args)
    return (outs[0], None) if last else (outs[0], outs[1])


def _retention_tables():
    c = RET_CHUNK
    lg = jnp.log(1.0 - 2.0 ** (-5.0 - jnp.arange(RET_HEADS, dtype=F32)))
    idx = jnp.arange(c, dtype=F32)
    rel = idx[:, None] - idx[None, :]
    inner = jnp.where(rel[None] >= 0, jnp.exp(lg[:, None, None] * rel[None]), 0.0)
    qdec = jnp.exp(lg[:, None] * (idx + 1.0))[:, :, None]
    kdec_t = jnp.exp(lg[:, None] * (c - 1.0 - idx))
    cdec = jnp.exp(lg * c)
    lane = np.arange(RET_HEADS * RET_DK)
    head_of_lane = lane // RET_DK
    hmask = jnp.asarray(head_of_lane[None, :] == np.arange(RET_HEADS)[:, None], F32)
    qmask = hmask[:, None, :]
    kdec = hmask[:, :, None] * kdec_t[:, None, :]
    cdec_b = jnp.broadcast_to(cdec[:, None, None], (RET_HEADS, 1, RET_DV))
    return qmask, kdec, qdec, inner, cdec_b


def kernel(x, p, positions, g_mix, w_in, g_qa, g_kva, w_q_up, w_kv_up, g_qn, g_kn, w_sc, w_cf,
           b_cf, g_cf_ln, b_cf_ln, g_ret, w_o, g_ffn, w_up, w_ffn_conv, w_down, g_pe, w_pe, w_pg):
    assert x.shape == (1, SEQ, D_MODEL)
    xs = x[0]
    posf = positions.astype(F32)
    rows = lambda a: a[:, None, :]

    mla_freq = (ROPE_BASE ** (-jnp.arange(0, MLA_ROPE, 2, dtype=F32) / MLA_ROPE))[:, None]
    ret_freq = (1.0 / (10000.0 ** jnp.linspace(0.0, 1.0, RET_DK // 2, dtype=F32)))[:, None]
    qmask, kdec, qdec, inner, cdec = _retention_tables()

    w_in_r = jnp.pad(w_in.astype(BF16), ((0, 0), (0, 0), (W_IN_SHIFT, 0)))
    wa = w_in[:, :, :MLA_IN].astype(BF16)
    wq = jnp.pad(w_q_up.reshape(DEPTH, Q_LORA, MLA_HEADS, MLA_QK),
                 ((0, 0), (0, 0), (0, 0), (0, HEAD_PAD - MLA_QK))
                 ).reshape(DEPTH, Q_LORA, MLA_HEADS * HEAD_PAD).astype(BF16)
    wkv4 = w_kv_up.reshape(DEPTH, KV_LORA, MLA_HEADS, MLA_NOPE + MLA_V)
    wkv = jnp.concatenate([wkv4[..., :MLA_NOPE].reshape(DEPTH, KV_LORA, -1),
                           wkv4[..., MLA_NOPE:].reshape(DEPTH, KV_LORA, -1)], axis=2).astype(BF16)
    gq = jnp.pad(g_qn * (MLA_QK ** -0.5 * LOG2_E), ((0, 0), (0, HEAD_PAD - MLA_QK)))
    gkr = jnp.pad(g_kn[:, MLA_NOPE:], ((0, 0), (0, LANES - MLA_ROPE)))
    gm, g_ret4 = rows(g_mix), g_ret[:, :, None, :]

    hm, rope = _prologue(xs, gm, posf, mla_freq, ret_freq)
    for l in range(DEPTH):
        qt, k, vt = _mla_pre(l, hm, rope, wa, rows(g_qa), rows(g_kva), wq, wkv, rows(gq),
                             rows(g_kn[:, :MLA_NOPE]), rows(gkr))
        y_a = _flash(qt, k, vt)
        first = l == 0
        y_b, wd, *cast_b = _sconv(l, hm, w_in_r, w_sc,
                                  cast=((w_down, l, 1, 0),) + ((w_o,) if first else ()))
        y_c, *cast_c = _conformer(l, hm, w_in_r, w_cf, rows(b_cf), rows(g_cf_ln), rows(b_cf_ln),
                                  cast=(w_pg, w_pe) if first else ())
        if first:
            (w_o_b,), (w_pg_b, w_pe_b) = cast_b, cast_c
        y_d, wg, wu = _retention(l, hm, rope, w_in_r, qmask, kdec, qdec, inner, cdec, g_ret4,
                                 cast=((w_up, l, 2, 0), (w_up, l, 2, 1)))
        xs, hf = _out_proj(l, xs, y_a, y_b, y_c, y_d, w_o_b, rows(g_ffn))
        xs = _ffn(l, xs, hf, wg, wu, wd, w_ffn_conv)
        xs, hm = _ple(l, xs, p, rows(g_pe), w_pg_b, w_pe_b, gm)
    return xs[None]
```
